```python
import jax, jax.numpy as jnp
from jax import lax
import numpy as np

D_MODEL = 1024
BATCH = 1
SEQ = 16384
DEPTH = 2
DEC_BATCH = 32
DEC_SEQ = 64
PAST_LEN = 4096

CHUNK = 64
N_META = 16
POOL_WINDOWS = (2, 4, 8, 16)
N_POOL_GROUPS = 4
POOL_GROUP = D_MODEL // N_POOL_GROUPS
POOL_STATE = max(POOL_WINDOWS) - 1
CONV_WIDTH = 3
D_FF = 4 * D_MODEL
EPS = 1e-5

kernel_name = "hybrid_pool_shortconv_stream_step"


def rmsnorm(x, g):
    xf = x.astype(jnp.float32)
    y = xf * lax.rsqrt(jnp.mean(xf * xf, axis=-1, keepdims=True) + EPS)
    return (y * g.astype(jnp.float32)).astype(x.dtype)


def pool_mixer(h, hist, pos0, w_pool, pool_scale):
    bsz, L, _ = h.shape
    P = hist.shape[1]
    u = jnp.concatenate([hist.astype(h.dtype), h], axis=1)
    N = P + L
    c = jnp.cumsum(u.astype(jnp.float32), axis=1)
    pos = pos0 + jnp.arange(L, dtype=jnp.int32)
    hf = h.astype(jnp.float32)
    diffs = []
    for g, w in enumerate(POOL_WINDOWS):
        sl = slice(g * POOL_GROUP, (g + 1) * POOL_GROUP)
        cpad = jnp.pad(c[..., sl], ((0, 0), (w, 0), (0, 0)))
        s = cpad[:, P + w:] - cpad[:, P:N]
        cnt = jnp.minimum(pos + 1, w).astype(jnp.float32)
        diffs.append(s / cnt[None, :, None] - hf[..., sl])
    d = jnp.stack(diffs, axis=2).astype(h.dtype)
    out = jnp.einsum('blgc,gcd->blgd', d, w_pool).reshape(bsz, L, D_MODEL)
    return out * pool_scale, u[:, N - POOL_STATE:]


def conv_mixer(h, hist, w_conv_in, conv_w, w_conv_out):
    L = h.shape[1]
    z = h @ w_conv_in
    b, c, v = jnp.split(z, 3, axis=-1)
    u = c * v
    uf = jnp.concatenate([hist.astype(u.dtype), u], axis=1)
    y = conv_w[0] * uf[:, 0:L]
    for k in range(1, CONV_WIDTH):
        y = y + conv_w[k] * uf[:, k:k + L]
    return (b * y) @ w_conv_out, uf[:, -(CONV_WIDTH - 1):]


def sqrelu_mlp(h, w_up, w_down):
    a = jax.nn.relu(h @ w_up)
    return (a * a) @ w_down


def trunk(x, pool_hist, conv_hist, pos0, norm_mix, norm_mlp, norm_final,
          w_pool, pool_scale, w_conv_in, conv_w, w_conv_out, w_up, w_down):
    new_pool = None
    new_conv = None
    for i in range(DEPTH):
        h = rmsnorm(x, norm_mix[i])
        if i % 2 == 0:
            dx, new_pool = pool_mixer(h, pool_hist, pos0, w_pool, pool_scale)
        else:
            dx, new_conv = conv_mixer(h, conv_hist, w_conv_in, conv_w, w_conv_out)
        x = x + dx
        x = x + sqrelu_mlp(rmsnorm(x, norm_mlp[i]), w_up[i], w_down[i])
    return rmsnorm(x, norm_final), new_pool, new_conv


def setup_inputs(seed: int = 0) -> dict:
    key = jax.random.key(seed)
    ks = jax.random.split(key, 16)
    f32 = jnp.float32
    x_prompt = jax.random.normal(ks[0], (BATCH, SEQ, D_MODEL), f32)
    x_sample = jax.random.normal(ks[1], (DEC_BATCH, DEC_SEQ, D_MODEL), f32)
    state_pool = jax.random.normal(ks[2], (DEC_BATCH, POOL_STATE, D_MODEL), f32)
    state_conv = jax.random.normal(ks[3], (DEC_BATCH, CONV_WIDTH - 1, D_MODEL), f32)
    meta_tokens = jax.random.normal(ks[4], (N_META, D_MODEL), f32)
    norm_mix = 1.0 + 0.02 * jax.random.normal(ks[5], (DEPTH, D_MODEL), f32)
    norm_mlp = 1.0 + 0.02 * jax.random.normal(ks[6], (DEPTH, D_MODEL), f32)
    norm_final = 1.0 + 0.02 * jax.random.normal(ks[7], (D_MODEL,), f32)
    w_pool = jax.random.normal(ks[8], (N_POOL_GROUPS, POOL_GROUP, POOL_GROUP), f32) * POOL_GROUP ** -0.5
    pool_scale = 1.0 + 0.02 * jax.random.normal(ks[9], (D_MODEL,), f32)
    w_conv_in = jax.random.normal(ks[10], (D_MODEL, 3 * D_MODEL), f32) * D_MODEL ** -0.5
    conv_w = jax.random.normal(ks[11], (CONV_WIDTH, D_MODEL), f32) * CONV_WIDTH ** -0.5
    w_conv_out = jax.random.normal(ks[12], (D_MODEL, D_MODEL), f32) * D_MODEL ** -0.5
    w_up = jax.random.normal(ks[13], (DEPTH, D_MODEL, D_FF), f32) * D_MODEL ** -0.5
    w_down = jax.random.normal(ks[14], (DEPTH, D_FF, D_MODEL), f32) * D_FF ** -0.5
    return {"x_prompt": x_prompt, "x_sample": x_sample,
            "state_pool": state_pool, "state_conv": state_conv,
            "meta_tokens": meta_tokens, "norm_mix": norm_mix, "norm_mlp": norm_mlp,
            "norm_final": norm_final, "w_pool": w_pool, "pool_scale": pool_scale,
            "w_conv_in": w_conv_in, "conv_w": conv_w, "w_conv_out": w_conv_out,
            "w_up": w_up, "w_down": w_down}


def reference(x_prompt, x_sample, state_pool, state_conv, meta_tokens, norm_mix, norm_mlp,
              norm_final, w_pool, pool_scale, w_conv_in, conv_w, w_conv_out, w_up, w_down):
    bsz = x_prompt.shape[0]
    meta = jnp.broadcast_to(meta_tokens.astype(x_prompt.dtype)[None], (bsz, N_META, D_MODEL))
    xp = jnp.concatenate([meta, x_prompt], axis=1)
    pool_hist0 = jnp.zeros((bsz, 0, D_MODEL), x_prompt.dtype)
    conv_hist0 = jnp.zeros((bsz, CONV_WIDTH - 1, D_MODEL), x_prompt.dtype)
    yp, new_pool_prompt, new_conv_prompt = trunk(
        xp, pool_hist0, conv_hist0, 0, norm_mix, norm_mlp, norm_final,
        w_pool, pool_scale, w_conv_in, conv_w, w_conv_out, w_up, w_down)
    y_prompt = yp[:, N_META:]
    y_sample, new_pool_sample, new_conv_sample = trunk(
        x_sample, state_pool, state_conv, PAST_LEN, norm_mix, norm_mlp, norm_final,
        w_pool, pool_scale, w_conv_in, conv_w, w_conv_out, w_up, w_down)
    return (y_prompt, y_sample, new_pool_prompt, new_pool_sample, new_conv_prompt, new_conv_sample)
```

```python
import functools

import jax
import jax.numpy as jnp
from jax import lax
from jax.experimental import pallas as pl
from jax.experimental.pallas import tpu as pltpu

D = 1024
F = 4 * D
WINDOWS = (2, 4, 8, 16)
NG = len(WINDOWS)
G = D // NG
PSTATE = max(WINDOWS) - 1
CW = 3
CSTATE = CW - 1
NMETA = 16
EPS = 1e-5

SEG = 64
HP = 16
HC = 8
TM = 256
NSEG = TM // SEG
FC = 1024
VMEM_LIMIT = 56 * 1024 * 1024

f32 = jnp.float32
bf16 = jnp.bfloat16


def _rmsnorm(x, g):
    ms = jnp.mean(x * x, axis=-1, keepdims=True)
    return x * lax.rsqrt(ms + EPS) * g


def _mlp(x, g, wup_ref, wdn_ref):
    hn = _rmsnorm(x, g).astype(bf16)
    acc = None
    for c in range(F // FC):
        a = jnp.dot(hn, wup_ref[:, c * FC:(c + 1) * FC], preferred_element_type=f32)
        a = jnp.maximum(a, 0.0)
        a = (a * a).astype(bf16)
        p = jnp.dot(a, wdn_ref[c * FC:(c + 1) * FC, :], preferred_element_type=f32)
        acc = p if acc is None else acc + p
    return acc


def _pool_sums(e_ref, base, rows, h_seg):
    out = []
    for g, w in enumerate(WINDOWS):
        sl = slice(g * G, (g + 1) * G)
        s = h_seg[:, sl]
        for k in range(1, w):
            s = s + e_ref[base + HP - k:base + HP - k + rows, sl]
        out.append(s)
    return out


def _layer0_tail(x, d, wpool_ref, pscale, gmlp, wup_ref, wdn_ref):
    db = d.astype(bf16)
    parts = [jnp.dot(db[:, g * G:(g + 1) * G], wpool_ref[g], preferred_element_type=f32)
             for g in range(NG)]
    x1 = x + jnp.concatenate(parts, axis=1) * pscale
    return x1 + _mlp(x1, gmlp, wup_ref, wdn_ref)


def _layer0_kernel(np_tiles, xp_ref, xs_ref, sp_ref, meta_ref, gmix_ref, gmlp_ref,
                   wpool_ref, pscale_ref, wup_ref, wdn_ref,
                   op_ref, os_ref, ometa_ref, npp_ref, nps_ref,
                   e_ref, em_ref, hc_ref):
    i = pl.program_id(0)
    is_prompt = i < np_tiles
    gmix = gmix_ref[...]
    gmlp = gmlp_ref[...]
    pscale = pscale_ref[...]
    tail = functools.partial(_layer0_tail, wpool_ref=wpool_ref, pscale=pscale, gmlp=gmlp,
                             wup_ref=wup_ref, wdn_ref=wdn_ref)

    @pl.when(i == 0)
    def _meta():
        xm = meta_ref[...]
        hm = _rmsnorm(xm, gmix)
        em_ref[0:HP, :] = jnp.zeros((HP, D), f32)
        em_ref[HP:HP + NMETA, :] = hm
        pos1 = lax.broadcasted_iota(jnp.int32, (NMETA, 1), 0) + 1
        sums = _pool_sums(em_ref, 0, NMETA, hm)
        cols = []
        for g, w in enumerate(WINDOWS):
            cnt = jnp.minimum(pos1, w).astype(f32)
            cols.append(sums[g] / cnt - hm[:, g * G:(g + 1) * G])
        ometa_ref[...] = tail(xm, jnp.concatenate(cols, axis=1))
        hc_ref[...] = hm

    x = jnp.where(is_prompt, xp_ref[...], xs_ref[...])
    h = _rmsnorm(x, gmix)
    for s in range(NSEG):
        e_ref[s * (HP + SEG) + HP:(s + 1) * (HP + SEG), :] = h[s * SEG:(s + 1) * SEG]

    @pl.when(is_prompt)
    def _prompt_hist():
        e_ref[0:HP, :] = hc_ref[...]
        for s in range(1, NSEG):
            e_ref[s * (HP + SEG):s * (HP + SEG) + HP, :] = h[s * SEG - HP:s * SEG]
        hc_ref[...] = h[TM - HP:TM]

    @pl.when(jnp.logical_not(is_prompt))
    def _sample_hist():
        for s in range(NSEG):
            b = s * (HP + SEG)
            e_ref[b + HP - PSTATE:b + HP, :] = sp_ref[s]

    rows = []
    for s in range(NSEG):
        h_seg = h[s * SEG:(s + 1) * SEG]
        sums = _pool_sums(e_ref, s * (HP + SEG), SEG, h_seg)
        rows.append(jnp.concatenate(
            [sums[g] * (1.0 / w) - h_seg[:, g * G:(g + 1) * G] for g, w in enumerate(WINDOWS)],
            axis=1))
    x2 = tail(x, jnp.concatenate(rows, axis=0))

    @pl.when(is_prompt)
    def _store_prompt():
        op_ref[...] = x2

    @pl.when(jnp.logical_not(is_prompt))
    def _store_sample():
        os_ref[...] = x2
        for s in range(NSEG):
            e = (s + 1) * (HP + SEG)
            nps_ref[s] = e_ref[e - PSTATE:e, :]

    @pl.when(i == np_tiles - 1)
    def _store_prompt_state():
        e = NSEG * (HP + SEG)
        npp_ref[0] = e_ref[e - PSTATE:e, :]


def _conv_u(h1b, wci_ref):
    c = jnp.dot(h1b, wci_ref[:, D:2 * D], preferred_element_type=f32)
    v = jnp.dot(h1b, wci_ref[:, 2 * D:3 * D], preferred_element_type=f32)
    return c * v


def _layer1_kernel(np_tiles, xp_ref, xs_ref, sc_ref, xmeta_ref, gmix_ref, gmlp_ref, gfin_ref,
                   wci_ref, cw_ref, wco_ref, wup_ref, wdn_ref,
                   yp_ref, ys_ref, ncp_ref, ncs_ref,
                   u_ref, uc_ref):
    i = pl.program_id(0)
    is_prompt = i < np_tiles
    gmix = gmix_ref[...]
    cw = cw_ref[...]

    @pl.when(i == 0)
    def _meta():
        hm = _rmsnorm(xmeta_ref[...], gmix).astype(bf16)
        um = _conv_u(hm, wci_ref)
        uc_ref[...] = um[NMETA - HC:NMETA]

    x = jnp.where(is_prompt, xp_ref[...], xs_ref[...])
    h1b = _rmsnorm(x, gmix).astype(bf16)
    u = _conv_u(h1b, wci_ref)
    for s in range(NSEG):
        u_ref[s * (HC + SEG) + HC:(s + 1) * (HC + SEG), :] = u[s * SEG:(s + 1) * SEG]

    @pl.when(is_prompt)
    def _prompt_hist():
        u_ref[0:HC, :] = uc_ref[...]
        for s in range(1, NSEG):
            u_ref[s * (HC + SEG):s * (HC + SEG) + HC, :] = u[s * SEG - HC:s * SEG]
        uc_ref[...] = u[TM - HC:TM]

    @pl.when(jnp.logical_not(is_prompt))
    def _sample_hist():
        for s in range(NSEG):
            b = s * (HC + SEG)
            u_ref[b + HC - CSTATE:b + HC, :] = sc_ref[s]

    rows = []
    for s in range(NSEG):
        b = s * (HC + SEG) + HC
        y = cw[CW - 1:CW, :] * u[s * SEG:(s + 1) * SEG]
        for k in range(1, CW):
            y = y + cw[CW - 1 - k:CW - k, :] * u_ref[b - k:b - k + SEG, :]
        rows.append(y)
    y = jnp.concatenate(rows, axis=0)
    bgate = jnp.dot(h1b, wci_ref[:, 0:D], preferred_element_type=f32)
    x3 = x + jnp.dot((bgate * y).astype(bf16), wco_ref[...], preferred_element_type=f32)
    x4 = x3 + _mlp(x3, gmlp_ref[...], wup_ref, wdn_ref)
    out = _rmsnorm(x4, gfin_ref[...])

    @pl.when(is_prompt)
    def _store_prompt():
        yp_ref[...] = out

    @pl.when(jnp.logical_not(is_prompt))
    def _store_sample():
        ys_ref[...] = out
        for s in range(NSEG):
            e = (s + 1) * (HC + SEG)
            ncs_ref[s] = u_ref[e - CSTATE:e, :]

    @pl.when(i == np_tiles - 1)
    def _store_prompt_state():
        e = NSEG * (HC + SEG)
        ncp_ref[0] = u_ref[e - CSTATE:e, :]


def _const_spec(shape, single=False):
    nd = len(shape)
    kw = {"pipeline_mode": pl.Buffered(1)} if single else {}
    return pl.BlockSpec(shape, lambda i: (0,) * nd, **kw)


def _layer_spec(shape):
    return lambda layer: pl.BlockSpec((None,) + shape, lambda i: (layer,) + (0,) * len(shape),
                                      pipeline_mode=pl.Buffered(1))


def kernel(x_prompt, x_sample, state_pool, state_conv, meta_tokens, norm_mix, norm_mlp,
           norm_final, w_pool, pool_scale, w_conv_in, conv_w, w_conv_out, w_up, w_down):
    bsz, seq, _ = x_prompt.shape
    nb, dseq, _ = x_sample.shape
    assert bsz == 1 and dseq == SEG and seq % TM == 0 and (nb * dseq) % TM == 0
    np_tiles = seq // TM
    ns_tiles = nb * dseq // TM
    grid = (np_tiles + ns_tiles,)

    xp = x_prompt.reshape(seq, D)
    xs = x_sample.reshape(nb * dseq, D)
    wup_b = w_up.astype(bf16)
    wdn_b = w_down.astype(bf16)
    wpool_b = w_pool.astype(bf16)
    wci_b = w_conv_in.astype(bf16)
    wco_b = w_conv_out.astype(bf16)
    row = lambda a: a.reshape(1, D)

    p_idx = lambda i: (jnp.minimum(i, np_tiles - 1), 0)
    s_idx = lambda i: (jnp.maximum(i - np_tiles, 0), 0)
    s_idx3 = lambda i: (jnp.maximum(i - np_tiles, 0), 0, 0)
    tile_p = pl.BlockSpec((TM, D), p_idx)
    tile_s = pl.BlockSpec((TM, D), s_idx)
    up_spec = _layer_spec((D, F))
    dn_spec = _layer_spec((F, D))
    params = pltpu.CompilerParams(dimension_semantics=("arbitrary",),
                                  vmem_limit_bytes=VMEM_LIMIT)

    x1p, x1s, xmeta1, new_pool_p, new_pool_s = pl.pallas_call(
        functools.partial(_layer0_kernel, np_tiles),
        grid=grid,
        in_specs=[tile_p, tile_s,
                  pl.BlockSpec((NSEG, PSTATE, D), s_idx3),
                  _const_spec((NMETA, D)), _const_spec((1, D)), _const_spec((1, D)),
                  _const_spec((NG, G, G), single=True), _const_spec((1, D)),
                  up_spec(0), dn_spec(0)],
        out_specs=[tile_p, tile_s, _const_spec((NMETA, D)),
                   _const_spec((1, PSTATE, D)),
                   pl.BlockSpec((NSEG, PSTATE, D), s_idx3)],
        out_shape=[jax.ShapeDtypeStruct((seq, D), f32),
                   jax.ShapeDtypeStruct((nb * dseq, D), f32),
                   jax.ShapeDtypeStruct((NMETA, D), f32),
                   jax.ShapeDtypeStruct((1, PSTATE, D), f32),
                   jax.ShapeDtypeStruct((nb, PSTATE, D), f32)],
        scratch_shapes=[pltpu.VMEM((NSEG * (HP + SEG), D), f32),
                        pltpu.VMEM((HP + NMETA, D), f32),
                        pltpu.VMEM((HP, D), f32)],
        compiler_params=params,
        name="layer0_pool_mlp",
    )(xp, xs, state_pool, meta_tokens, row(norm_mix[0]), row(norm_mlp[0]),
      wpool_b, row(pool_scale), wup_b, wdn_b)

    yp, ys, new_conv_p, new_conv_s = pl.pallas_call(
        functools.partial(_layer1_kernel, np_tiles),
        grid=grid,
        in_specs=[tile_p, tile_s,
                  pl.BlockSpec((NSEG, CSTATE, D), s_idx3),
                  _const_spec((NMETA, D)), _const_spec((1, D)), _const_spec((1, D)),
                  _const_spec((1, D)),
                  _const_spec((D, 3 * D), single=True), _const_spec((CW, D)),
                  _const_spec((D, D), single=True),
                  up_spec(1), dn_spec(1)],
        out_specs=[tile_p, tile_s,
                   _const_spec((1, CSTATE, D)),
                   pl.BlockSpec((NSEG, CSTATE, D), s_idx3)],
        out_shape=[jax.ShapeDtypeStruct((seq, D), f32),
                   jax.ShapeDtypeStruct((nb * dseq, D), f32),
                   jax.ShapeDtypeStruct((1, CSTATE, D), f32),
                   jax.ShapeDtypeStruct((nb, CSTATE, D), f32)],
        scratch_shapes=[pltpu.VMEM((NSEG * (HC + SEG), D), f32),
                        pltpu.VMEM((HC, D), f32)],
        compiler_params=params,
        name="layer1_conv_mlp",
    )(x1p, x1s, state_conv, xmeta1, row(norm_mix[1]), row(norm_mlp[1]), row(norm_final),
      wci_b, conv_w, wco_b, wup_b, wdn_b)

    return (yp.reshape(1, seq, D), ys.reshape(nb, dseq, D),
            new_pool_p, new_pool_s, new_conv_p, new_conv_s)
```

```python
import functools

import jax
import jax.numpy as jnp
from jax import lax
from jax.experimental import pallas as pl
from jax.experimental.pallas import tpu as pltpu

D = 1024
F = 4 * D
WINDOWS = (2, 4, 8, 16)
NG = len(WINDOWS)
G = D // NG
PSTATE = max(WINDOWS) - 1
CW = 3
CSTATE = CW - 1
NMETA = 16
EPS = 1e-5

SEG = 64
HP = 16
HC = 8
TM = 512
NSEG = TM // SEG
FC = 1024
NCH = F // FC
WN = 512
VMEM_LIMIT = 60 * 1024 * 1024

f32 = jnp.float32
bf16 = jnp.bfloat16


def _rmsnorm(x, g):
    ms = jnp.mean(x * x, axis=-1, keepdims=True)
    return x * lax.rsqrt(ms + EPS) * g


def _weight_blocks(w_hbm, w_scr, lead=()):
    k, n = w_hbm.shape[len(lead):]
    ppb = D // WN
    return [(w_hbm.at[lead + (pl.ds(r * TM, TM), pl.ds(c * D, D))],
             [w_scr.at[c * ppb + p, pl.ds(r * TM, TM), :] for p in range(ppb)])
            for c in range(n // D) for r in range(k // TM)]


def _stage_weights(blocks, stg_ref, sem_ref):
    def copy(k):
        return pltpu.make_async_copy(blocks[k][0], stg_ref.at[k % 2], sem_ref.at[k % 2])

    copy(0).start()
    for k, (_, panels) in enumerate(blocks):
        if k + 1 < len(blocks):
            copy(k + 1).start()
        copy(k).wait()
        for p, dst in enumerate(panels):
            dst[...] = stg_ref[k % 2, :, p * WN:(p + 1) * WN].astype(bf16)


def _dot_panels(x, w_ref, panels, rows=slice(None)):
    return jnp.concatenate(
        [jnp.dot(x, w_ref[j, rows, :], preferred_element_type=f32) for j in panels], axis=1)


def _mlp(x, g, wup_ref, wdn_ref):
    hn = _rmsnorm(x, g).astype(bf16)
    ppc = FC // WN
    acc = x
    for c in range(NCH):
        a = _dot_panels(hn, wup_ref, range(c * ppc, (c + 1) * ppc))
        a = jnp.maximum(a, 0.0)
        a = (a * a).astype(bf16)
        acc = acc + _dot_panels(a, wdn_ref, range(D // WN), slice(c * FC, (c + 1) * FC))
    return acc


def _pool_sums(hist, h_seg):
    out = []
    for g, w in enumerate(WINDOWS):
        sl = slice(g * G, (g + 1) * G)
        s = jnp.concatenate([hist[:, sl], h_seg[:, sl]], axis=0)
        shift = 1
        while shift < w:
            s = s + pltpu.roll(s, shift, 0)
            shift *= 2
        out.append(s[HP:])
    return out


def _pool_proj(x, db, wpool_ref, pscale):
    parts = [jnp.dot(db[:, g * G:(g + 1) * G], wpool_ref[g], preferred_element_type=f32)
             for g in range(NG)]
    return x + jnp.concatenate(parts, axis=1) * pscale


def _layer0_kernel(ns_tiles, nt_tiles,
                   xs_ref, xp_ref, sp_ref, meta_ref, gmix_ref, gmlp_ref,
                   wpool_f32_ref, pscale_ref, wup_hbm, wdn_hbm,
                   os_ref, op_ref, ometa_ref, nps_ref, npp_ref,
                   tail_ref, st_ref, hc_ref, hmeta_ref,
                   wpool_ref, wup_ref, wdn_ref, stg_ref, sem_ref):
    t = pl.program_id(0)
    is_sample = t < ns_tiles
    gmix = gmix_ref[...]
    gmlp = gmlp_ref[...]
    pscale = pscale_ref[...]

    @pl.when(t == 0)
    def _init():
        _stage_weights(_weight_blocks(wup_hbm, wup_ref, (0,)) +
                       _weight_blocks(wdn_hbm, wdn_ref, (0,)), stg_ref, sem_ref)
        wpool_ref[...] = wpool_f32_ref[...].astype(bf16)
        xm = meta_ref[...]
        hm = _rmsnorm(xm, gmix)
        pos1 = lax.broadcasted_iota(jnp.int32, (NMETA, 1), 0) + 1
        sums = _pool_sums(jnp.zeros((HP, D), f32), hm)
        cols = []
        for g, w in enumerate(WINDOWS):
            cnt = jnp.minimum(pos1, w).astype(f32)
            cols.append(sums[g] / cnt - hm[:, g * G:(g + 1) * G])
        d = jnp.concatenate(cols, axis=1).astype(bf16)
        ometa_ref[...] = _mlp(_pool_proj(xm, d, wpool_ref, pscale), gmlp, wup_ref, wdn_ref)
        hmeta_ref[...] = hm
        st_ref[...] = jnp.zeros(st_ref.shape, f32)
        hc_ref[...] = jnp.zeros(hc_ref.shape, f32)

    x = jnp.where(is_sample, xs_ref[...], xp_ref[...])
    h = _rmsnorm(x, gmix)
    carry = jnp.where(t == ns_tiles, hmeta_ref[...], hc_ref[...])
    hc_ref[...] = h[TM - HP:TM]
    rows = []
    for s in range(NSEG):
        h_seg = h[s * SEG:(s + 1) * SEG]
        st_ref[s * HP + HP - PSTATE:(s + 1) * HP, :] = sp_ref[s]
        hist = jnp.where(is_sample, st_ref[s * HP:(s + 1) * HP, :],
                         carry if s == 0 else h[s * SEG - HP:s * SEG])
        tail_ref[s * HP:(s + 1) * HP, :] = h_seg[SEG - HP:SEG]
        sums = _pool_sums(hist, h_seg)
        rows.append(jnp.concatenate(
            [sums[g] * (1.0 / w) - h_seg[:, g * G:(g + 1) * G] for g, w in enumerate(WINDOWS)],
            axis=1).astype(bf16))
    x1 = _pool_proj(x, jnp.concatenate(rows, axis=0), wpool_ref, pscale)
    op_ref[...] = _mlp(x1, gmlp, wup_ref, wdn_ref)

    @pl.when(is_sample)
    def _store_sample():
        os_ref[...] = op_ref[...]
        for s in range(NSEG):
            nps_ref[s] = tail_ref[(s + 1) * HP - PSTATE:(s + 1) * HP, :]

    @pl.when(t == nt_tiles - 1)
    def _store_prompt_state():
        npp_ref[0] = tail_ref[NSEG * HP - PSTATE:NSEG * HP, :]


def _layer1_kernel(ns_tiles, nt_tiles,
                   xs_ref, xp_ref, sc_ref, xmeta_ref, gmix_ref, gmlp_ref, gfin_ref,
                   wci_hbm, cw_ref, wco_hbm, wup_hbm, wdn_hbm,
                   ys_ref, yp_ref, ncs_ref, ncp_ref,
                   u_ref, st_ref, uc_ref, umeta_ref,
                   wci_ref, wco_ref, wup_ref, wdn_ref, stg_ref, sem_ref):
    t = pl.program_id(0)
    is_sample = t < ns_tiles
    gmix = gmix_ref[...]
    cw = cw_ref[...]

    def conv_in(h1b, k):
        ppd = D // WN
        return _dot_panels(h1b, wci_ref, range(k * ppd, (k + 1) * ppd))

    @pl.when(t == 0)
    def _init():
        _stage_weights(_weight_blocks(wci_hbm, wci_ref) + _weight_blocks(wco_hbm, wco_ref) +
                       _weight_blocks(wup_hbm, wup_ref, (1,)) +
                       _weight_blocks(wdn_hbm, wdn_ref, (1,)), stg_ref, sem_ref)
        hm = _rmsnorm(xmeta_ref[...], gmix).astype(bf16)
        um = conv_in(hm, 1) * conv_in(hm, 2)
        umeta_ref[...] = um[NMETA - HC:NMETA]
        st_ref[...] = jnp.zeros(st_ref.shape, f32)
        uc_ref[...] = jnp.zeros(uc_ref.shape, f32)

    def conv_segment(s, u, prev):
        b = s * (HC + SEG)
        u_seg = u[s * SEG:(s + 1) * SEG]
        st_ref[s * HC + HC - CSTATE:(s + 1) * HC, :] = sc_ref[s]
        u_ref[b:b + HC, :] = jnp.where(is_sample, st_ref[s * HC:(s + 1) * HC, :], prev)
        u_ref[b + HC:b + HC + SEG, :] = u_seg
        y = cw[CW - 1:CW, :] * u_seg
        for k in range(1, CW):
            y = y + cw[CW - 1 - k:CW - k, :] * u_ref[b + HC - k:b + HC - k + SEG, :]
        return y

    x = jnp.where(is_sample, xs_ref[...], xp_ref[...])
    h1b = _rmsnorm(x, gmix).astype(bf16)
    u = conv_in(h1b, 1) * conv_in(h1b, 2)
    carry = jnp.where(t == ns_tiles, umeta_ref[...], uc_ref[...])
    uc_ref[...] = u[TM - HC:TM]
    rows = [conv_segment(s, u, carry if s == 0 else u[s * SEG - HC:s * SEG])
            for s in range(NSEG)]
    gated = (conv_in(h1b, 0) * jnp.concatenate(rows, axis=0)).astype(bf16)
    x3 = x + _dot_panels(gated, wco_ref, range(D // WN))
    yp_ref[...] = _rmsnorm(_mlp(x3, gmlp_ref[...], wup_ref, wdn_ref), gfin_ref[...])

    @pl.when(is_sample)
    def _store_sample():
        ys_ref[...] = yp_ref[...]
        for s in range(NSEG):
            e = (s + 1) * (HC + SEG)
            ncs_ref[s] = u_ref[e - CSTATE:e, :]

    @pl.when(t == nt_tiles - 1)
    def _store_prompt_state():
        e = NSEG * (HC + SEG)
        ncp_ref[0] = u_ref[e - CSTATE:e, :]


def _const_spec(shape, single=False):
    nd = len(shape)
    kw = {"pipeline_mode": pl.Buffered(1)} if single else {}
    return pl.BlockSpec(shape, lambda i: (0,) * nd, **kw)


_HBM_SPEC = pl.BlockSpec(memory_space=pl.ANY)


def _panel_scratch(k, n):
    return pltpu.VMEM((n // WN, k, WN), bf16)


_STAGING = [pltpu.VMEM((2, TM, D), f32), pltpu.SemaphoreType.DMA((2,))]


def kernel(x_prompt, x_sample, state_pool, state_conv, meta_tokens, norm_mix, norm_mlp,
           norm_final, w_pool, pool_scale, w_conv_in, conv_w, w_conv_out, w_up, w_down):
    bsz, seq, _ = x_prompt.shape
    nb, dseq, _ = x_sample.shape
    assert bsz == 1 and dseq == SEG and seq % TM == 0 and (nb * dseq) % TM == 0
    np_tiles = seq // TM
    ns_tiles = nb * dseq // TM
    nt_tiles = ns_tiles + np_tiles

    xp = x_prompt.reshape(seq, D)
    xs = x_sample.reshape(nb * dseq, D)
    row = lambda a: a.reshape(1, D)

    s_idx = lambda t: (jnp.minimum(t, ns_tiles - 1), 0)
    p_idx = lambda t: (jnp.maximum(t - ns_tiles, 0), 0)
    s_idx3 = lambda t: (jnp.minimum(t, ns_tiles - 1), 0, 0)
    xs_spec = pl.BlockSpec((TM, D), s_idx, pipeline_mode=pl.Buffered(1))
    tile_s = pl.BlockSpec((TM, D), s_idx)
    tile_p = pl.BlockSpec((TM, D), p_idx)
    params = pltpu.CompilerParams(dimension_semantics=("arbitrary",),
                                  vmem_limit_bytes=VMEM_LIMIT)

    x1s, x1p, xmeta1, new_pool_s, new_pool_p = pl.pallas_call(
        functools.partial(_layer0_kernel, ns_tiles, nt_tiles),
        grid=(nt_tiles,),
        in_specs=[xs_spec, tile_p,
                  pl.BlockSpec((NSEG, PSTATE, D), s_idx3),
                  _const_spec((NMETA, D)), _const_spec((1, D)), _const_spec((1, D)),
                  _const_spec((NG, G, G), single=True), _const_spec((1, D)),
                  _HBM_SPEC, _HBM_SPEC],
        out_specs=[tile_s, tile_p, _const_spec((NMETA, D)),
                   pl.BlockSpec((NSEG, PSTATE, D), s_idx3),
                   _const_spec((1, PSTATE, D))],
        out_shape=[jax.ShapeDtypeStruct((nb * dseq, D), f32),
                   jax.ShapeDtypeStruct((seq, D), f32),
                   jax.ShapeDtypeStruct((NMETA, D), f32),
                   jax.ShapeDtypeStruct((nb, PSTATE, D), f32),
                   jax.ShapeDtypeStruct((1, PSTATE, D), f32)],
        scratch_shapes=[pltpu.VMEM((NSEG * HP, D), f32),
                        pltpu.VMEM((NSEG * HP, D), f32),
                        pltpu.VMEM((HP, D), f32),
                        pltpu.VMEM((HP, D), f32),
                        pltpu.VMEM((NG, G, G), bf16),
                        _panel_scratch(D, F), _panel_scratch(F, D)] + _STAGING,
        compiler_params=params,
        name="layer0_pool_mlp",
    )(xs, xp, state_pool, meta_tokens, row(norm_mix[0]), row(norm_mlp[0]),
      w_pool, row(pool_scale), w_up, w_down)

    ys, yp, new_conv_s, new_conv_p = pl.pallas_call(
        functools.partial(_layer1_kernel, ns_tiles, nt_tiles),
        grid=(nt_tiles,),
        in_specs=[xs_spec, tile_p,
                  pl.BlockSpec((NSEG, CSTATE, D), s_idx3),
                  _const_spec((NMETA, D)), _const_spec((1, D)), _const_spec((1, D)),
                  _const_spec((1, D)),
                  _HBM_SPEC, _const_spec((CW, D)), _HBM_SPEC, _HBM_SPEC, _HBM_SPEC],
        out_specs=[tile_s, tile_p,
                   pl.BlockSpec((NSEG, CSTATE, D), s_idx3),
                   _const_spec((1, CSTATE, D))],
        out_shape=[jax.ShapeDtypeStruct((nb * dseq, D), f32),
                   jax.ShapeDtypeStruct((seq, D), f32),
                   jax.ShapeDtypeStruct((nb, CSTATE, D), f32),
                   jax.ShapeDtypeStruct((1, CSTATE, D), f32)],
        scratch_shapes=[pltpu.VMEM((NSEG * (HC + SEG), D), f32),
                        pltpu.VMEM((NSEG * HC, D), f32),
                        pltpu.VMEM((HC, D), f32),
                        pltpu.VMEM((HC, D), f32),
                        _panel_scratch(D, 3 * D), _panel_scratch(D, D),
                        _panel_scratch(D, F), _panel_scratch(F, D)] + _STAGING,
        compiler_params=params,
        name="layer1_conv_mlp",
    )(x1s, x1p, state_conv, xmeta1, row(norm_mix[1]), row(norm_mlp[1]), row(norm_final),
      w_conv_in, conv_w, w_conv_out, w_up, w_down)

    return (yp.reshape(1, seq, D), ys.reshape(nb, dseq, D),
            new_pool_p, new_pool_s, new_conv_p, new_conv_s)
```

```python
import functools

import jax
import jax.numpy as jnp
from jax import lax
from jax.experimental import pallas as pl
from jax.experimental.pallas import tpu as pltpu

D = 1024
F = 4 * D
WINDOWS = (2, 4, 8, 16)
NG = len(WINDOWS)
G = D // NG
PSTATE = max(WINDOWS) - 1
CW = 3
CSTATE = CW - 1
NMETA = 16
EPS = 1e-5

SEG = 64
HP = 16
HC = 8
TM = 512
NSEG = TM // SEG
FC = 1024
NCH = F // FC
WN = 512
VMEM_LIMIT = 60 * 1024 * 1024

f32 = jnp.float32
bf16 = jnp.bfloat16


def _rmsnorm(x, g):
    ms = jnp.mean(x * x, axis=-1, keepdims=True)
    return x * lax.rsqrt(ms + EPS) * g


def _weight_blocks(w_hbm, w_scr, lead=()):
    k, n = w_hbm.shape[len(lead):]
    ppb = D // WN
    return [(w_hbm.at[lead + (pl.ds(r * TM, TM), pl.ds(c * D, D))],
             [w_scr.at[c * ppb + p, pl.ds(r * TM, TM), :] for p in range(ppb)])
            for c in range(n // D) for r in range(k // TM)]


def _stage_weights(blocks, stg_ref, sem_ref):
    def copy(k):
        return pltpu.make_async_copy(blocks[k][0], stg_ref.at[k % 2], sem_ref.at[k % 2])

    copy(0).start()
    for k, (_, panels) in enumerate(blocks):
        if k + 1 < len(blocks):
            copy(k + 1).start()
        copy(k).wait()
        for p, dst in enumerate(panels):
            dst[...] = stg_ref[k % 2, :, p * WN:(p + 1) * WN].astype(bf16)


def _dot_panels(x, w_ref, panels, rows=slice(None)):
    return jnp.concatenate(
        [jnp.dot(x, w_ref[j, rows, :], preferred_element_type=f32) for j in panels], axis=1)


def _mlp(x, g, wup_ref, wdn_ref):
    hn = _rmsnorm(x, g).astype(bf16)
    ppc = FC // WN
    acc = x
    for c in range(NCH):
        a = _dot_panels(hn, wup_ref, range(c * ppc, (c + 1) * ppc))
        a = jnp.maximum(a, 0.0)
        a = (a * a).astype(bf16)
        acc = acc + _dot_panels(a, wdn_ref, range(D // WN), slice(c * FC, (c + 1) * FC))
    return acc


def _pool_sums(hist, h_seg):
    out = []
    for g, w in enumerate(WINDOWS):
        sl = slice(g * G, (g + 1) * G)
        s = jnp.concatenate([hist[:, sl], h_seg[:, sl]], axis=0)
        shift = 1
        while shift < w:
            s = s + pltpu.roll(s, shift, 0)
            shift *= 2
        out.append(s[HP:])
    return out


def _pool_proj(x, db, wpool_ref, pscale):
    parts = [jnp.dot(db[:, g * G:(g + 1) * G], wpool_ref[g], preferred_element_type=f32)
             for g in range(NG)]
    return x + jnp.concatenate(parts, axis=1) * pscale


def _layer0_kernel(ns_tiles, nt_tiles,
                   xs_ref, xp_ref, sp_ref, meta_ref, gmix_ref, gmlp_ref,
                   wpool_f32_ref, pscale_ref, wup_hbm, wdn_hbm,
                   os_ref, op_ref, ometa_ref, nps_ref, npp_ref,
                   tail_ref, st_ref, hc_ref, hmeta_ref,
                   wpool_ref, wup_ref, wdn_ref, stg_ref, sem_ref):
    t = pl.program_id(0)
    is_sample = t < ns_tiles
    gmix = gmix_ref[0:1, :]
    gmlp = gmlp_ref[0:1, :]
    pscale = pscale_ref[...]

    @pl.when(t == 0)
    def _init():
        _stage_weights(_weight_blocks(wup_hbm, wup_ref, (0,)) +
                       _weight_blocks(wdn_hbm, wdn_ref, (0,)), stg_ref, sem_ref)
        wpool_ref[...] = wpool_f32_ref[...].astype(bf16)
        xm = meta_ref[...]
        hm = _rmsnorm(xm, gmix)
        pos1 = lax.broadcasted_iota(jnp.int32, (NMETA, 1), 0) + 1
        sums = _pool_sums(jnp.zeros((HP, D), f32), hm)
        cols = []
        for g, w in enumerate(WINDOWS):
            cnt = jnp.minimum(pos1, w).astype(f32)
            cols.append(sums[g] / cnt - hm[:, g * G:(g + 1) * G])
        d = jnp.concatenate(cols, axis=1).astype(bf16)
        ometa_ref[...] = _mlp(_pool_proj(xm, d, wpool_ref, pscale), gmlp, wup_ref, wdn_ref)
        hmeta_ref[...] = hm
        st_ref[...] = jnp.zeros(st_ref.shape, f32)
        hc_ref[...] = jnp.zeros(hc_ref.shape, f32)

    @pl.when(is_sample)
    def _load_sample_state():
        for s in range(NSEG):
            st_ref[s * HP + HP - PSTATE:(s + 1) * HP, :] = sp_ref[:, s, :]

    x = jnp.where(is_sample, xs_ref[...], xp_ref[...])
    h = _rmsnorm(x, gmix)
    carry = jnp.where(t == ns_tiles, hmeta_ref[...], hc_ref[...])
    hc_ref[...] = h[TM - HP:TM]
    rows = []
    for s in range(NSEG):
        h_seg = h[s * SEG:(s + 1) * SEG]
        hist = jnp.where(is_sample, st_ref[s * HP:(s + 1) * HP, :],
                         carry if s == 0 else h[s * SEG - HP:s * SEG])
        tail_ref[s * HP:(s + 1) * HP, :] = h_seg[SEG - HP:SEG]
        sums = _pool_sums(hist, h_seg)
        rows.append(jnp.concatenate(
            [sums[g] * (1.0 / w) - h_seg[:, g * G:(g + 1) * G] for g, w in enumerate(WINDOWS)],
            axis=1).astype(bf16))
    x1 = _pool_proj(x, jnp.concatenate(rows, axis=0), wpool_ref, pscale)
    op_ref[...] = _mlp(x1, gmlp, wup_ref, wdn_ref)

    @pl.when(is_sample)
    def _store_sample():
        os_ref[...] = op_ref[...]
        for s in range(NSEG):
            nps_ref[:, s, :] = tail_ref[(s + 1) * HP - PSTATE:(s + 1) * HP, :]

    @pl.when(t == nt_tiles - 1)
    def _store_prompt_state():
        npp_ref[:, 0, :] = tail_ref[NSEG * HP - PSTATE:NSEG * HP, :]


def _layer1_kernel(ns_tiles, nt_tiles,
                   xs_ref, xp_ref, sc_ref, xmeta_ref, gmix_ref, gmlp_ref, gfin_ref,
                   wci_hbm, cw_ref, wco_hbm, wup_hbm, wdn_hbm,
                   ys_ref, yp_ref, ncs_ref, ncp_ref,
                   u_ref, st_ref, uc_ref, umeta_ref,
                   wci_ref, wco_ref, wup_ref, wdn_ref, stg_ref, sem_ref):
    t = pl.program_id(0)
    is_sample = t < ns_tiles
    gmix = gmix_ref[1:2, :]
    cw = cw_ref[...]

    def conv_in(h1b, k):
        ppd = D // WN
        return _dot_panels(h1b, wci_ref, range(k * ppd, (k + 1) * ppd))

    @pl.when(t == 0)
    def _init():
        _stage_weights(_weight_blocks(wci_hbm, wci_ref) + _weight_blocks(wco_hbm, wco_ref) +
                       _weight_blocks(wup_hbm, wup_ref, (1,)) +
                       _weight_blocks(wdn_hbm, wdn_ref, (1,)), stg_ref, sem_ref)
        hm = _rmsnorm(xmeta_ref[...], gmix).astype(bf16)
        um = conv_in(hm, 1) * conv_in(hm, 2)
        umeta_ref[...] = um[NMETA - HC:NMETA]
        st_ref[...] = jnp.zeros(st_ref.shape, f32)
        uc_ref[...] = jnp.zeros(uc_ref.shape, f32)

    def conv_segment(s, u, prev):
        b = s * (HC + SEG)
        u_seg = u[s * SEG:(s + 1) * SEG]
        st_ref[s * HC + HC - CSTATE:(s + 1) * HC, :] = sc_ref[s]
        u_ref[b:b + HC, :] = jnp.where(is_sample, st_ref[s * HC:(s + 1) * HC, :], prev)
        u_ref[b + HC:b + HC + SEG, :] = u_seg
        y = cw[CW - 1:CW, :] * u_seg
        for k in range(1, CW):
            y = y + cw[CW - 1 - k:CW - k, :] * u_ref[b + HC - k:b + HC - k + SEG, :]
        return y

    x = jnp.where(is_sample, xs_ref[...], xp_ref[...])
    h1b = _rmsnorm(x, gmix).astype(bf16)
    u = conv_in(h1b, 1) * conv_in(h1b, 2)
    carry = jnp.where(t == ns_tiles, umeta_ref[...], uc_ref[...])
    uc_ref[...] = u[TM - HC:TM]
    rows = [conv_segment(s, u, carry if s == 0 else u[s * SEG - HC:s * SEG])
            for s in range(NSEG)]
    gated = (conv_in(h1b, 0) * jnp.concatenate(rows, axis=0)).astype(bf16)
    x3 = x + _dot_panels(gated, wco_ref, range(D // WN))
    yp_ref[...] = _rmsnorm(_mlp(x3, gmlp_ref[1:2, :], wup_ref, wdn_ref), gfin_ref[...])

    @pl.when(is_sample)
    def _store_sample():
        ys_ref[...] = yp_ref[...]
        for s in range(NSEG):
            e = (s + 1) * (HC + SEG)
            ncs_ref[s] = u_ref[e - CSTATE:e, :]

    @pl.when(t == nt_tiles - 1)
    def _store_prompt_state():
        e = NSEG * (HC + SEG)
        ncp_ref[0] = u_ref[e - CSTATE:e, :]


def _const_spec(shape, single=False):
    nd = len(shape)
    kw = {"pipeline_mode": pl.Buffered(1)} if single else {}
    return pl.BlockSpec(shape, lambda i: (0,) * nd, **kw)


_HBM_SPEC = pl.BlockSpec(memory_space=pl.ANY)


def _panel_scratch(k, n):
    return pltpu.VMEM((n // WN, k, WN), bf16)


_STAGING = [pltpu.VMEM((2, TM, D), f32), pltpu.SemaphoreType.DMA((2,))]


def kernel(x_prompt, x_sample, state_pool, state_conv, meta_tokens, norm_mix, norm_mlp,
           norm_final, w_pool, pool_scale, w_conv_in, conv_w, w_conv_out, w_up, w_down):
    bsz, seq, _ = x_prompt.shape
    nb, dseq, _ = x_sample.shape
    assert bsz == 1 and dseq == SEG and seq % TM == 0 and (nb * dseq) % TM == 0
    np_tiles = seq // TM
    ns_tiles = nb * dseq // TM
    nt_tiles = ns_tiles + np_tiles

    xp = x_prompt.reshape(seq, D)
    xs = x_sample.reshape(nb * dseq, D)
    row = lambda a: a.reshape(1, D)

    s_idx = lambda t: (jnp.minimum(t, ns_tiles - 1), 0)
    p_idx = lambda t: (jnp.maximum(t - ns_tiles, 0), 0)
    s_idx3 = lambda t: (jnp.minimum(t, ns_tiles - 1), 0, 0)
    ps_idx = lambda t: (0, jnp.minimum(t, ns_tiles - 1), 0)
    pool_state_t = jnp.swapaxes(state_pool, 0, 1)
    nlayers = norm_mix.shape[0]
    xs_spec = pl.BlockSpec((TM, D), s_idx, pipeline_mode=pl.Buffered(1))
    tile_s = pl.BlockSpec((TM, D), s_idx)
    tile_p = pl.BlockSpec((TM, D), p_idx)
    params = pltpu.CompilerParams(dimension_semantics=("arbitrary",),
                                  vmem_limit_bytes=VMEM_LIMIT)

    x1s, x1p, xmeta1, new_pool_s, new_pool_p = pl.pallas_call(
        functools.partial(_layer0_kernel, ns_tiles, nt_tiles),
        grid=(nt_tiles,),
        in_specs=[xs_spec, tile_p,
                  pl.BlockSpec((PSTATE, NSEG, D), ps_idx),
                  _const_spec((NMETA, D)), _const_spec((nlayers, D)), _const_spec((nlayers, D)),
                  _const_spec((NG, G, G), single=True), _const_spec((1, D)),
                  _HBM_SPEC, _HBM_SPEC],
        out_specs=[tile_s, tile_p, _const_spec((NMETA, D)),
                   pl.BlockSpec((PSTATE, NSEG, D), ps_idx),
                   _const_spec((PSTATE, 1, D))],
        out_shape=[jax.ShapeDtypeStruct((nb * dseq, D), f32),
                   jax.ShapeDtypeStruct((seq, D), f32),
                   jax.ShapeDtypeStruct((NMETA, D), f32),
                   jax.ShapeDtypeStruct((PSTATE, nb, D), f32),
                   jax.ShapeDtypeStruct((PSTATE, 1, D), f32)],
        scratch_shapes=[pltpu.VMEM((NSEG * HP, D), f32),
                        pltpu.VMEM((NSEG * HP, D), f32),
                        pltpu.VMEM((HP, D), f32),
                        pltpu.VMEM((HP, D), f32),
                        pltpu.VMEM((NG, G, G), bf16),
                        _panel_scratch(D, F), _panel_scratch(F, D)] + _STAGING,
        compiler_params=params,
        name="layer0_pool_mlp",
    )(xs, xp, pool_state_t, meta_tokens, norm_mix, norm_mlp,
      w_pool, row(pool_scale), w_up, w_down)

    ys, yp, new_conv_s, new_conv_p = pl.pallas_call(
        functools.partial(_layer1_kernel, ns_tiles, nt_tiles),
        grid=(nt_tiles,),
        in_specs=[xs_spec, tile_p,
                  pl.BlockSpec((NSEG, CSTATE, D), s_idx3),
                  _const_spec((NMETA, D)), _const_spec((nlayers, D)), _const_spec((nlayers, D)),
                  _const_spec((1, D)),
                  _HBM_SPEC, _const_spec((CW, D)), _HBM_SPEC, _HBM_SPEC, _HBM_SPEC],
        out_specs=[tile_s, tile_p,
                   pl.BlockSpec((NSEG, CSTATE, D), s_idx3),
                   _const_spec((1, CSTATE, D))],
        out_shape=[jax.ShapeDtypeStruct((nb * dseq, D), f32),
                   jax.ShapeDtypeStruct((seq, D), f32),
                   jax.ShapeDtypeStruct((nb, CSTATE, D), f32),
                   jax.ShapeDtypeStruct((1, CSTATE, D), f32)],
        scratch_shapes=[pltpu.VMEM((NSEG * (HC + SEG), D), f32),
                        pltpu.VMEM((NSEG * HC, D), f32),
                        pltpu.VMEM((HC, D), f32),
                        pltpu.VMEM((HC, D), f32),
                        _panel_scratch(D, 3 * D), _panel_scratch(D, D),
                        _panel_scratch(D, F), _panel_scratch(F, D)] + _STAGING,
        compiler_params=params,
        name="layer1_conv_mlp",
    )(x1s, x1p, state_conv, xmeta1, norm_mix, norm_mlp, row(norm_final),
      w_conv_in, conv_w, w_conv_out, w_up, w_down)

    return (yp.reshape(1, seq, D), ys.reshape(nb, dseq, D),
            jnp.swapaxes(new_pool_p, 0, 1), jnp.swapaxes(new_pool_s, 0, 1),
            new_conv_p, new_conv_s)
```

```python
import functools

import jax
import jax.numpy as jnp
from jax import lax
from jax.experimental import pallas as pl
from jax.experimental.pallas import tpu as pltpu

D = 1024
F = 4 * D
WINDOWS = (2, 4, 8, 16)
NG = len(WINDOWS)
G = D // NG
PSTATE = max(WINDOWS) - 1
CW = 3
CSTATE = CW - 1
NMETA = 16
EPS = 1e-5

SEG = 64
HP = 16
HC = 8
TM = 512
NSEG = TM // SEG
FC = 1024
NCH = F // FC
WN = 512
VMEM_LIMIT = 60 * 1024 * 1024

f32 = jnp.float32
bf16 = jnp.bfloat16


def _rmsnorm(x, g):
    ms = jnp.mean(x * x, axis=-1, keepdims=True)
    return x * lax.rsqrt(ms + EPS) * g


def _weight_blocks(w_hbm, w_scr, lead=()):
    k, n = w_hbm.shape[len(lead):]
    ppb = D // WN
    return [(w_hbm.at[lead + (pl.ds(r * TM, TM), pl.ds(c * D, D))],
             [w_scr.at[c * ppb + p, pl.ds(r * TM, TM), :] for p in range(ppb)])
            for c in range(n // D) for r in range(k // TM)]


def _stage_weights(blocks, stg_ref, sem_ref):
    def copy(k):
        return pltpu.make_async_copy(blocks[k][0], stg_ref.at[k % 2], sem_ref.at[k % 2])

    copy(0).start()
    for k, (_, panels) in enumerate(blocks):
        if k + 1 < len(blocks):
            copy(k + 1).start()
        copy(k).wait()
        for p, dst in enumerate(panels):
            dst[...] = stg_ref[k % 2, :, p * WN:(p + 1) * WN].astype(bf16)


def _dot_panels(x, w_ref, panels, rows=slice(None)):
    return jnp.concatenate(
        [jnp.dot(x, w_ref[j, rows, :], preferred_element_type=f32) for j in panels], axis=1)


def _mlp(x, g, wup_ref, wdn_ref):
    hn = _rmsnorm(x, g).astype(bf16)
    ppc = FC // WN
    acc = x
    for c in range(NCH):
        a = _dot_panels(hn, wup_ref, range(c * ppc, (c + 1) * ppc))
        a = jnp.maximum(a, 0.0)
        a = (a * a).astype(bf16)
        acc = acc + _dot_panels(a, wdn_ref, range(D // WN), slice(c * FC, (c + 1) * FC))
    return acc


def _pool_sums(hist, h_seg):
    out = []
    for g, w in enumerate(WINDOWS):
        sl = slice(g * G, (g + 1) * G)
        s = jnp.concatenate([hist[:, sl], h_seg[:, sl]], axis=0)
        shift = 1
        while shift < w:
            s = s + pltpu.roll(s, shift, 0)
            shift *= 2
        out.append(s[HP:])
    return out


def _pool_proj(x, db, wpool_ref, pscale):
    parts = [jnp.dot(db[:, g * G:(g + 1) * G], wpool_ref[g], preferred_element_type=f32)
             for g in range(NG)]
    return x + jnp.concatenate(parts, axis=1) * pscale


def _layer0_kernel(ns_tiles, nt_tiles,
                   xs_ref, xp_ref, sp_ref, meta_ref, gmix_ref, gmlp_ref,
                   wpool_f32_ref, pscale_ref, wup_hbm, wdn_hbm,
                   os_ref, op_ref, ometa_ref, nps_ref, npp_ref,
                   tail_ref, st_ref, hc_ref, hmeta_ref,
                   wpool_ref, wup_ref, wdn_ref, stg_ref, sem_ref):
    t = pl.program_id(0)
    is_sample = t < ns_tiles
    gmix = gmix_ref[0:1, :]
    gmlp = gmlp_ref[0:1, :]
    pscale = pscale_ref[...]

    @pl.when(t == 0)
    def _init():
        _stage_weights(_weight_blocks(wup_hbm, wup_ref, (0,)) +
                       _weight_blocks(wdn_hbm, wdn_ref, (0,)), stg_ref, sem_ref)
        wpool_ref[...] = wpool_f32_ref[...].astype(bf16)
        xm = meta_ref[...]
        hm = _rmsnorm(xm, gmix)
        pos1 = lax.broadcasted_iota(jnp.int32, (NMETA, 1), 0) + 1
        sums = _pool_sums(jnp.zeros((HP, D), f32), hm)
        cols = []
        for g, w in enumerate(WINDOWS):
            cnt = jnp.minimum(pos1, w).astype(f32)
            cols.append(sums[g] / cnt - hm[:, g * G:(g + 1) * G])
        d = jnp.concatenate(cols, axis=1).astype(bf16)
        ometa_ref[...] = _mlp(_pool_proj(xm, d, wpool_ref, pscale), gmlp, wup_ref, wdn_ref)
        hmeta_ref[...] = hm
        st_ref[...] = jnp.zeros(st_ref.shape, f32)
        hc_ref[...] = jnp.zeros(hc_ref.shape, f32)

    @pl.when(is_sample)
    def _load_sample_state():
        for s in range(NSEG):
            st_ref[s * HP + HP - PSTATE:(s + 1) * HP, :] = sp_ref[:, s, :]

    x = jnp.where(is_sample, xs_ref[...], xp_ref[...])
    h = _rmsnorm(x, gmix)
    carry = jnp.where(t == ns_tiles, hmeta_ref[...], hc_ref[...])
    hc_ref[...] = h[TM - HP:TM]
    rows = []
    for s in range(NSEG):
        h_seg = h[s * SEG:(s + 1) * SEG]
        hist = jnp.where(is_sample, st_ref[s * HP:(s + 1) * HP, :],
                         carry if s == 0 else h[s * SEG - HP:s * SEG])
        tail_ref[s * HP:(s + 1) * HP, :] = h_seg[SEG - HP:SEG]
        sums = _pool_sums(hist, h_seg)
        rows.append(jnp.concatenate(
            [sums[g] * (1.0 / w) - h_seg[:, g * G:(g + 1) * G] for g, w in enumerate(WINDOWS)],
            axis=1).astype(bf16))
    x1 = _pool_proj(x, jnp.concatenate(rows, axis=0), wpool_ref, pscale)
    op_ref[...] = _mlp(x1, gmlp, wup_ref, wdn_ref)

    @pl.when(is_sample)
    def _store_sample():
        os_ref[...] = op_ref[...]
        for s in range(NSEG):
            nps_ref[:, s, :] = tail_ref[(s + 1) * HP - PSTATE:(s + 1) * HP, :]

    @pl.when(t == nt_tiles - 1)
    def _store_prompt_state():
        npp_ref[:, 0, :] = tail_ref[NSEG * HP - PSTATE:NSEG * HP, :]


def _layer1_kernel(ns_tiles, nt_tiles,
                   xs_ref, xp_ref, sc_ref, xmeta_ref, gmix_ref, gmlp_ref, gfin_ref,
                   wci_hbm, cw_ref, wco_hbm, wup_hbm, wdn_hbm,
                   ys_ref, yp_ref, ncs_ref, ncp_ref,
                   u_ref, st_ref, uc_ref, umeta_ref,
                   wci_ref, wco_ref, wup_ref, wdn_ref, stg_ref, sem_ref):
    t = pl.program_id(0)
    is_sample = t < ns_tiles
    gmix = gmix_ref[1:2, :]
    cw = cw_ref[...]

    def conv_in(h1b, k):
        ppd = D // WN
        return _dot_panels(h1b, wci_ref, range(k * ppd, (k + 1) * ppd))

    @pl.when(t == 0)
    def _init():
        _stage_weights(_weight_blocks(wci_hbm, wci_ref) + _weight_blocks(wco_hbm, wco_ref) +
                       _weight_blocks(wup_hbm, wup_ref, (1,)) +
                       _weight_blocks(wdn_hbm, wdn_ref, (1,)), stg_ref, sem_ref)
        hm = _rmsnorm(xmeta_ref[...], gmix).astype(bf16)
        um = conv_in(hm, 1) * conv_in(hm, 2)
        umeta_ref[...] = um[NMETA - HC:NMETA]
        st_ref[...] = jnp.zeros(st_ref.shape, f32)
        uc_ref[...] = jnp.zeros(uc_ref.shape, f32)

    def conv_segment(s, u, prev):
        b = s * (HC + SEG)
        u_seg = u[s * SEG:(s + 1) * SEG]
        st_ref[s * HC + HC - CSTATE:(s + 1) * HC, :] = sc_ref[s]
        u_ref[b:b + HC, :] = jnp.where(is_sample, st_ref[s * HC:(s + 1) * HC, :], prev)
        u_ref[b + HC:b + HC + SEG, :] = u_seg
        y = cw[CW - 1:CW, :] * u_seg
        for k in range(1, CW):
            y = y + cw[CW - 1 - k:CW - k, :] * u_ref[b + HC - k:b + HC - k + SEG, :]
        return y

    x = jnp.where(is_sample, xs_ref[...], xp_ref[...])
    h1b = _rmsnorm(x, gmix).astype(bf16)
    u = conv_in(h1b, 1) * conv_in(h1b, 2)
    carry = jnp.where(t == ns_tiles, umeta_ref[...], uc_ref[...])
    uc_ref[...] = u[TM - HC:TM]
    rows = [conv_segment(s, u, carry if s == 0 else u[s * SEG - HC:s * SEG])
            for s in range(NSEG)]
    gated = (conv_in(h1b, 0) * jnp.concatenate(rows, axis=0)).astype(bf16)
    x3 = x + _dot_panels(gated, wco_ref, range(D // WN))
    yp_ref[...] = _rmsnorm(_mlp(x3, gmlp_ref[1:2, :], wup_ref, wdn_ref), gfin_ref[...])

    @pl.when(is_sample)
    def _store_sample():
        ys_ref[...] = yp_ref[...]
        for s in range(NSEG):
            e = (s + 1) * (HC + SEG)
            ncs_ref[s] = u_ref[e - CSTATE:e, :]

    @pl.when(t == nt_tiles - 1)
    def _store_prompt_state():
        e = NSEG * (HC + SEG)
        ncp_ref[0] = u_ref[e - CSTATE:e, :]


def _const_spec(shape, single=False):
    nd = len(shape)
    kw = {"pipeline_mode": pl.Buffered(1)} if single else {}
    return pl.BlockSpec(shape, lambda i: (0,) * nd, **kw)


_HBM_SPEC = pl.BlockSpec(memory_space=pl.ANY)


def _panel_scratch(k, n):
    return pltpu.VMEM((n // WN, k, WN), bf16)


_STAGING = [pltpu.VMEM((2, TM, D), f32), pltpu.SemaphoreType.DMA((2,))]


def kernel(x_prompt, x_sample, state_pool, state_conv, meta_tokens, norm_mix, norm_mlp,
           norm_final, w_pool, pool_scale, w_conv_in, conv_w, w_conv_out, w_up, w_down):
    bsz, seq, _ = x_prompt.shape
    nb, dseq, _ = x_sample.shape
    assert bsz == 1 and dseq == SEG and seq % TM == 0 and (nb * dseq) % TM == 0
    np_tiles = seq // TM
    ns_tiles = nb * dseq // TM
    nt_tiles = ns_tiles + np_tiles

    xp = x_prompt.reshape(seq, D)
    xs = x_sample.reshape(nb * dseq, D)
    row = lambda a: a.reshape(1, D)

    s_idx = lambda t: (jnp.minimum(t, ns_tiles - 1), 0)
    p_idx = lambda t: (jnp.maximum(t - ns_tiles, 0), 0)
    s_idx3 = lambda t: (jnp.minimum(t, ns_tiles - 1), 0, 0)
    ps_idx = lambda t: (0, jnp.minimum(t, ns_tiles - 1), 0)
    pool_state_t = jnp.swapaxes(state_pool, 0, 1)
    nlayers = norm_mix.shape[0]
    tile_s = pl.BlockSpec((TM, D), s_idx)
    tile_p = pl.BlockSpec((TM, D), p_idx)
    params = pltpu.CompilerParams(dimension_semantics=("arbitrary",),
                                  vmem_limit_bytes=VMEM_LIMIT)

    x1s, x1p, xmeta1, new_pool_s, new_pool_p = pl.pallas_call(
        functools.partial(_layer0_kernel, ns_tiles, nt_tiles),
        grid=(nt_tiles,),
        in_specs=[tile_s, tile_p,
                  pl.BlockSpec((PSTATE, NSEG, D), ps_idx),
                  _const_spec((NMETA, D)), _const_spec((nlayers, D)), _const_spec((nlayers, D)),
                  _const_spec((NG, G, G), single=True), _const_spec((1, D)),
                  _HBM_SPEC, _HBM_SPEC],
        out_specs=[tile_s, tile_p, _const_spec((NMETA, D)),
                   pl.BlockSpec((PSTATE, NSEG, D), ps_idx),
                   _const_spec((PSTATE, 1, D))],
        out_shape=[jax.ShapeDtypeStruct((nb * dseq, D), f32),
                   jax.ShapeDtypeStruct((seq, D), f32),
                   jax.ShapeDtypeStruct((NMETA, D), f32),
                   jax.ShapeDtypeStruct((PSTATE, nb, D), f32),
                   jax.ShapeDtypeStruct((PSTATE, 1, D), f32)],
        scratch_shapes=[pltpu.VMEM((NSEG * HP, D), f32),
                        pltpu.VMEM((NSEG * HP, D), f32),
                        pltpu.VMEM((HP, D), f32),
                        pltpu.VMEM((HP, D), f32),
                        pltpu.VMEM((NG, G, G), bf16),
                        _panel_scratch(D, F), _panel_scratch(F, D)] + _STAGING,
        compiler_params=params,
        name="layer0_pool_mlp",
    )(xs, xp, pool_state_t, meta_tokens, norm_mix, norm_mlp,
      w_pool, row(pool_scale), w_up, w_down)

    ys, yp, new_conv_s, new_conv_p = pl.pallas_call(
        functools.partial(_layer1_kernel, ns_tiles, nt_tiles),
        grid=(nt_tiles,),
        in_specs=[tile_s, tile_p,
                  pl.BlockSpec((NSEG, CSTATE, D), s_idx3),
                  _const_spec((NMETA, D)), _const_spec((nlayers, D)), _const_spec((nlayers, D)),
                  _const_spec((1, D)),
                  _HBM_SPEC, _const_spec((CW, D)), _HBM_SPEC, _HBM_SPEC, _HBM_SPEC],
        out_specs=[tile_s, tile_p,
                   pl.BlockSpec((NSEG, CSTATE, D), s_idx3),
                   _const_spec((1, CSTATE, D))],
        out_shape=[jax.ShapeDtypeStruct((nb * dseq, D), f32),
                   jax.ShapeDtypeStruct((seq, D), f32),
                   jax.ShapeDtypeStruct((nb, CSTATE, D), f32),
                   jax.ShapeDtypeStruct((1, CSTATE, D), f32)],
        scratch_shapes=[pltpu.VMEM((NSEG * (HC + SEG), D), f32),
                        pltpu.VMEM((NSEG * HC, D), f32),
                        pltpu.VMEM((HC, D), f32),
                        pltpu.VMEM((HC, D), f32),
                        _panel_scratch(D, 3 * D), _panel_scratch(D, D),
                        _panel_scratch(D, F), _panel_scratch(F, D)] + _STAGING,
        compiler_params=params,
        name="layer1_conv_mlp",
    )(x1s, x1p, state_conv, xmeta1, norm_mix, norm_mlp, row(norm_final),
      w_conv_in, conv_w, w_conv_out, w_up, w_down)

    return (yp.reshape(1, seq, D), ys.reshape(nb, dseq, D),
            jnp.swapaxes(new_pool_p, 0, 1), jnp.swapaxes(new_pool_s, 0, 1),
            new_conv_p, new_conv_s)
```

```python
import functools

import jax
import jax.numpy as jnp
from jax import lax
from jax.experimental import pallas as pl
from jax.experimental.pallas import tpu as pltpu

D = 1024
F = 4 * D
WINDOWS = (2, 4, 8, 16)
NG = len(WINDOWS)
G = D // NG
PSTATE = max(WINDOWS) - 1
CW = 3
CSTATE = CW - 1
NMETA = 16
EPS = 1e-5

SEG = 64
NSEG = 8
TM = SEG * NSEG
HP = 16
FC = 1024
NCH = F // FC
WN = 512
VMEM_LIMIT = 60 * 1024 * 1024

f32 = jnp.float32
bf16 = jnp.bfloat16


def _rmsnorm(x, g):
    ms = jnp.mean(x * x, axis=-1, keepdims=True)
    return x * lax.rsqrt(ms + EPS) * g


def _weight_blocks(w_hbm, w_scr, lead=()):
    k, n = w_hbm.shape[len(lead):]
    ppb = D // WN
    return [(w_hbm.at[lead + (pl.ds(r * TM, TM), pl.ds(c * D, D))],
             [w_scr.at[c * ppb + p, pl.ds(r * TM, TM), :] for p in range(ppb)])
            for c in range(n // D) for r in range(k // TM)]


def _stage_weights(blocks, stg_ref, sem_ref):
    def copy(k):
        return pltpu.make_async_copy(blocks[k][0], stg_ref.at[k % 2], sem_ref.at[k % 2])

    copy(0).start()
    for k, (_, panels) in enumerate(blocks):
        if k + 1 < len(blocks):
            copy(k + 1).start()
        copy(k).wait()
        for p, dst in enumerate(panels):
            dst[...] = stg_ref[k % 2, :, p * WN:(p + 1) * WN].astype(bf16)


class _TileIO:
    def __init__(self, ns_tiles, nt_tiles, xs_hbm, xp_hbm, ys_hbm, yp_hbm,
                 xin_ref, yout_ref, sem_in, sem_out):
        self.ns, self.nt = ns_tiles, nt_tiles
        self.src, self.dst = (xs_hbm, xp_hbm), (ys_hbm, yp_hbm)
        self.xin, self.yout, self.sem_in, self.sem_out = xin_ref, yout_ref, sem_in, sem_out

    def _in(self, kind, idx, slot):
        return [pltpu.make_async_copy(self.src[kind].at[idx, b], self.xin.at[slot, :, b, :],
                                      self.sem_in.at[slot]) for b in range(NSEG)]

    def _out(self, kind, idx, slot):
        return [pltpu.make_async_copy(self.yout.at[slot, :, b, :], self.dst[kind].at[idx, b],
                                      self.sem_out.at[slot]) for b in range(NSEG)]

    def _for_tile(self, make, tile, slot, action):
        @pl.when(tile < self.ns)
        def _():
            for c in make(0, tile, slot):
                action(c)

        @pl.when(jnp.logical_and(tile >= self.ns, tile < self.nt))
        def _():
            for c in make(1, tile - self.ns, slot):
                action(c)

    def start_fetch(self, tile):
        self._for_tile(self._in, tile, tile % 2, lambda c: c.start())

    def wait_fetch(self, tile):
        for c in self._in(1, 0, tile % 2):
            c.wait()

    def start_store(self, tile):
        self._for_tile(self._out, tile, tile % 2, lambda c: c.start())

    def wait_store(self, tile):
        for c in self._out(1, 0, tile % 2):
            c.wait()


def _dot_panels(x, w_ref, panels, rows=slice(None)):
    return jnp.concatenate(
        [jnp.dot(x, w_ref[j, rows, :], preferred_element_type=f32) for j in panels], axis=1)


def _mlp(x, g, wup_ref, wdn_ref):
    hn = _rmsnorm(x, g).astype(bf16)
    ppc = FC // WN
    acc = x
    for c in range(NCH):
        a = _dot_panels(hn, wup_ref, range(c * ppc, (c + 1) * ppc))
        a = jnp.maximum(a, 0.0)
        a = (a * a).astype(bf16)
        acc = acc + _dot_panels(a, wdn_ref, range(D // WN), slice(c * FC, (c + 1) * FC))
    return acc


def _shift_steps(x, k, stride):
    return pltpu.roll(x, k * stride, 0)


def _window_sums(e, stride):
    out = []
    for g, w in enumerate(WINDOWS):
        s = e[:, g * G:(g + 1) * G]
        k = 1
        while k < w:
            s = s + _shift_steps(s, k, stride)
            k *= 2
        out.append(s)
    return out


def _prev_segment(tail, carry):
    steps = tail.shape[0] // NSEG
    t3 = pltpu.roll(tail.reshape(steps, NSEG, D), 1, 1)
    c3 = pltpu.roll(carry.reshape(steps, NSEG, D), 1, 1)
    first = lax.broadcasted_iota(jnp.int32, (steps, NSEG, D), 1) == 0
    return jnp.where(first, c3, t3).reshape(steps * NSEG, D)


def _pool_proj(x, db, wpool_ref, pscale):
    parts = [jnp.dot(db[:, g * G:(g + 1) * G], wpool_ref[g], preferred_element_type=f32)
             for g in range(NG)]
    return x + jnp.concatenate(parts, axis=1) * pscale


def _run_tile(t, nt_tiles, io, compute):
    io.wait_fetch(t)
    io.start_fetch(t + 1)
    compute(t % 2)
    io.start_store(t)

    @pl.when(t >= 1)
    def _():
        io.wait_store(t - 1)

    @pl.when(t == nt_tiles - 1)
    def _():
        io.wait_store(t)


def _layer0_kernel(ns_tiles, nt_tiles,
                   xs_hbm, xp_hbm, sp_ref, meta_ref, gmix_ref, gmlp_ref,
                   wpool_f32_ref, pscale_ref, wup_hbm, wdn_hbm,
                   os_hbm, op_hbm, ometa_ref, nps_ref, npp_ref,
                   xin_ref, yout_ref, st_ref, hc_ref, cmeta_ref,
                   wpool_ref, wup_ref, wdn_ref, stg_ref, sem_in, sem_out, sem_w):
    t = pl.program_id(0)
    is_sample = t < ns_tiles
    gmix = gmix_ref[0:1, :]
    gmlp = gmlp_ref[0:1, :]
    pscale = pscale_ref[...]
    io = _TileIO(ns_tiles, nt_tiles, xs_hbm, xp_hbm, os_hbm, op_hbm,
                 xin_ref, yout_ref, sem_in, sem_out)

    @pl.when(t == 0)
    def _init():
        io.start_fetch(t)
        _stage_weights(_weight_blocks(wup_hbm, wup_ref, (0,)) +
                       _weight_blocks(wdn_hbm, wdn_ref, (0,)), stg_ref, sem_w)
        wpool_ref[...] = wpool_f32_ref[...].astype(bf16)
        xm = meta_ref[...]
        hm = _rmsnorm(xm, gmix)
        pos1 = lax.broadcasted_iota(jnp.int32, (NMETA, 1), 0) + 1
        sums = _window_sums(jnp.concatenate([jnp.zeros((HP, D), f32), hm], axis=0), 1)
        cols = []
        for g, w in enumerate(WINDOWS):
            cnt = jnp.minimum(pos1, w).astype(f32)
            cols.append(sums[g][HP:] / cnt - hm[:, g * G:(g + 1) * G])
        d = jnp.concatenate(cols, axis=1).astype(bf16)
        ometa_ref[...] = _mlp(_pool_proj(xm, d, wpool_ref, pscale), gmlp, wup_ref, wdn_ref)
        cmeta_ref[...] = jnp.zeros(cmeta_ref.shape, f32)
        for k in range(HP):
            cmeta_ref[k * NSEG + NSEG - 1:(k + 1) * NSEG, :] = hm[NMETA - HP + k:NMETA - HP + k + 1]
        st_ref[...] = jnp.zeros(st_ref.shape, f32)
        hc_ref[...] = jnp.zeros(hc_ref.shape, f32)

    def compute(slot):
        x = xin_ref[slot].reshape(TM, D)
        h = _rmsnorm(x, gmix)
        tail = h[TM - HP * NSEG:TM]
        carry = jnp.where(t == ns_tiles, cmeta_ref[...], hc_ref[...])
        hc_ref[...] = tail
        st_ref[NSEG:HP * NSEG, :] = sp_ref[...].reshape(PSTATE * NSEG, D)
        hist = jnp.where(is_sample, st_ref[...], _prev_segment(tail, carry))
        sums = _window_sums(jnp.concatenate([hist, h], axis=0), NSEG)
        d = jnp.concatenate(
            [sums[g][HP * NSEG:] * (1.0 / w) - h[:, g * G:(g + 1) * G]
             for g, w in enumerate(WINDOWS)], axis=1).astype(bf16)
        x1 = _pool_proj(x, d, wpool_ref, pscale)
        yout_ref[slot] = _mlp(x1, gmlp, wup_ref, wdn_ref).reshape(SEG, NSEG, D)

    _run_tile(t, nt_tiles, io, compute)

    @pl.when(is_sample)
    def _store_sample_state():
        nps_ref[...] = hc_ref[(HP - PSTATE) * NSEG:HP * NSEG, :].reshape(PSTATE, NSEG, D)

    @pl.when(t == nt_tiles - 1)
    def _store_prompt_state():
        for k in range(PSTATE):
            r = (HP - PSTATE + k) * NSEG + NSEG - 1
            npp_ref[k] = hc_ref[r:r + 1, :]


def _layer1_kernel(ns_tiles, nt_tiles,
                   xs_hbm, xp_hbm, sc_ref, xmeta_ref, gmix_ref, gmlp_ref, gfin_ref,
                   wci_hbm, cw_ref, wco_hbm, wup_hbm, wdn_hbm,
                   ys_hbm, yp_hbm, ncs_ref, ncp_ref,
                   xin_ref, yout_ref, uc_ref, cmeta_ref,
                   wci_ref, wco_ref, wup_ref, wdn_ref, stg_ref, sem_in, sem_out, sem_w):
    t = pl.program_id(0)
    is_sample = t < ns_tiles
    gmix = gmix_ref[1:2, :]
    cw = cw_ref[...]
    io = _TileIO(ns_tiles, nt_tiles, xs_hbm, xp_hbm, ys_hbm, yp_hbm,
                 xin_ref, yout_ref, sem_in, sem_out)

    def conv_in(h1b, k):
        ppd = D // WN
        return _dot_panels(h1b, wci_ref, range(k * ppd, (k + 1) * ppd))

    @pl.when(t == 0)
    def _init():
        io.start_fetch(t)
        _stage_weights(_weight_blocks(wci_hbm, wci_ref) + _weight_blocks(wco_hbm, wco_ref) +
                       _weight_blocks(wup_hbm, wup_ref, (1,)) +
                       _weight_blocks(wdn_hbm, wdn_ref, (1,)), stg_ref, sem_w)
        hm = _rmsnorm(xmeta_ref[...], gmix).astype(bf16)
        um = conv_in(hm, 1) * conv_in(hm, 2)
        cmeta_ref[...] = jnp.zeros(cmeta_ref.shape, f32)
        for k in range(CSTATE):
            cmeta_ref[k * NSEG + NSEG - 1:(k + 1) * NSEG, :] = um[NMETA - CSTATE + k:NMETA - CSTATE + k + 1]
        uc_ref[...] = jnp.zeros(uc_ref.shape, f32)

    def compute(slot):
        x = xin_ref[slot].reshape(TM, D)
        h1b = _rmsnorm(x, gmix).astype(bf16)
        u = conv_in(h1b, 1) * conv_in(h1b, 2)
        tail = u[TM - CSTATE * NSEG:TM]
        carry = jnp.where(t == ns_tiles, cmeta_ref[...], uc_ref[...])
        uc_ref[...] = tail
        state = jnp.concatenate([sc_ref[:, k, :] for k in range(CSTATE)], axis=0)
        hist = jnp.where(is_sample, state, _prev_segment(tail, carry))
        e = jnp.concatenate([hist, u], axis=0)
        y = cw[CW - 1:CW, :] * u
        for k in range(1, CW):
            first = (CSTATE - k) * NSEG
            y = y + cw[CW - 1 - k:CW - k, :] * e[first:first + TM]
        gated = (conv_in(h1b, 0) * y).astype(bf16)
        x3 = x + _dot_panels(gated, wco_ref, range(D // WN))
        out = _rmsnorm(_mlp(x3, gmlp_ref[1:2, :], wup_ref, wdn_ref), gfin_ref[...])
        yout_ref[slot] = out.reshape(SEG, NSEG, D)

    _run_tile(t, nt_tiles, io, compute)

    @pl.when(is_sample)
    def _store_sample_state():
        for k in range(CSTATE):
            ncs_ref[:, k, :] = uc_ref[k * NSEG:(k + 1) * NSEG, :]

    @pl.when(t == nt_tiles - 1)
    def _store_prompt_state():
        for k in range(CSTATE):
            r = k * NSEG + NSEG - 1
            ncp_ref[0, k:k + 1, :] = uc_ref[r:r + 1, :]


def _const_spec(shape, single=False):
    nd = len(shape)
    kw = {"pipeline_mode": pl.Buffered(1)} if single else {}
    return pl.BlockSpec(shape, lambda i: (0,) * nd, **kw)


_HBM_SPEC = pl.BlockSpec(memory_space=pl.ANY)


def _panel_scratch(k, n):
    return pltpu.VMEM((n // WN, k, WN), bf16)


_TILE_BUFS = [pltpu.VMEM((2, SEG, NSEG, D), f32), pltpu.VMEM((2, SEG, NSEG, D), f32)]
_STAGING = [pltpu.VMEM((2, TM, D), f32)]
_SEMS = [pltpu.SemaphoreType.DMA((2,))] * 3


def kernel(x_prompt, x_sample, state_pool, state_conv, meta_tokens, norm_mix, norm_mlp,
           norm_final, w_pool, pool_scale, w_conv_in, conv_w, w_conv_out, w_up, w_down):
    bsz, seq, _ = x_prompt.shape
    nb, dseq, _ = x_sample.shape
    assert bsz == 1 and dseq == SEG and seq % TM == 0 and nb % NSEG == 0
    np_tiles = seq // TM
    ns_tiles = nb // NSEG
    nt_tiles = ns_tiles + np_tiles
    nlayers = norm_mix.shape[0]

    xs = x_sample.reshape(ns_tiles, NSEG, SEG, D)
    xp = x_prompt.reshape(np_tiles, NSEG, SEG, D)
    s_tiles = jax.ShapeDtypeStruct(xs.shape, f32)
    p_tiles = jax.ShapeDtypeStruct(xp.shape, f32)
    row = lambda a: a.reshape(1, D)
    st_idx = lambda t: (0, jnp.minimum(t, ns_tiles - 1), 0)
    cs_idx = lambda t: (jnp.minimum(t, ns_tiles - 1), 0, 0)
    pool_state_t = jnp.swapaxes(state_pool, 0, 1)
    params = pltpu.CompilerParams(dimension_semantics=("arbitrary",),
                                  vmem_limit_bytes=VMEM_LIMIT)

    x1s, x1p, xmeta1, new_pool_s, new_pool_p = pl.pallas_call(
        functools.partial(_layer0_kernel, ns_tiles, nt_tiles),
        grid=(nt_tiles,),
        in_specs=[_HBM_SPEC, _HBM_SPEC,
                  pl.BlockSpec((PSTATE, NSEG, D), st_idx),
                  _const_spec((NMETA, D)), _const_spec((nlayers, D)), _const_spec((nlayers, D)),
                  _const_spec((NG, G, G), single=True), _const_spec((1, D)),
                  _HBM_SPEC, _HBM_SPEC],
        out_specs=[_HBM_SPEC, _HBM_SPEC, _const_spec((NMETA, D)),
                   pl.BlockSpec((PSTATE, NSEG, D), st_idx),
                   _const_spec((PSTATE, 1, D))],
        out_shape=[s_tiles, p_tiles,
                   jax.ShapeDtypeStruct((NMETA, D), f32),
                   jax.ShapeDtypeStruct((PSTATE, nb, D), f32),
                   jax.ShapeDtypeStruct((PSTATE, 1, D), f32)],
        scratch_shapes=_TILE_BUFS +
                       [pltpu.VMEM((HP * NSEG, D), f32),
                        pltpu.VMEM((HP * NSEG, D), f32),
                        pltpu.VMEM((HP * NSEG, D), f32),
                        pltpu.VMEM((NG, G, G), bf16),
                        _panel_scratch(D, F), _panel_scratch(F, D)] + _STAGING + _SEMS,
        compiler_params=params,
        name="layer0_pool_mlp",
    )(xs, xp, pool_state_t, meta_tokens, norm_mix, norm_mlp,
      w_pool, row(pool_scale), w_up, w_down)

    ys, yp, new_conv_s, new_conv_p = pl.pallas_call(
        functools.partial(_layer1_kernel, ns_tiles, nt_tiles),
        grid=(nt_tiles,),
        in_specs=[_HBM_SPEC, _HBM_SPEC,
                  pl.BlockSpec((NSEG, CSTATE, D), cs_idx),
                  _const_spec((NMETA, D)), _const_spec((nlayers, D)), _const_spec((nlayers, D)),
                  _const_spec((1, D)),
                  _HBM_SPEC, _const_spec((CW, D)), _HBM_SPEC, _HBM_SPEC, _HBM_SPEC],
        out_specs=[_HBM_SPEC, _HBM_SPEC,
                   pl.BlockSpec((NSEG, CSTATE, D), cs_idx),
                   _const_spec((1, CSTATE, D))],
        out_shape=[s_tiles, p_tiles,
                   jax.ShapeDtypeStruct((nb, CSTATE, D), f32),
                   jax.ShapeDtypeStruct((1, CSTATE, D), f32)],
        scratch_shapes=_TILE_BUFS +
                       [pltpu.VMEM((CSTATE * NSEG, D), f32),
                        pltpu.VMEM((CSTATE * NSEG, D), f32),
                        _panel_scratch(D, 3 * D), _panel_scratch(D, D),
                        _panel_scratch(D, F), _panel_scratch(F, D)] + _STAGING + _SEMS,
        compiler_params=params,
        name="layer1_conv_mlp",
    )(x1s, x1p, state_conv, xmeta1, norm_mix, norm_mlp, row(norm_final),
      w_conv_in, conv_w, w_conv_out, w_up, w_down)

    return (yp.reshape(1, seq, D), ys.reshape(nb, dseq, D),
            jnp.swapaxes(new_pool_p, 0, 1), jnp.swapaxes(new_pool_s, 0, 1),
            new_conv_p, new_conv_s)
```

```python
import functools

import jax
import jax.numpy as jnp
from jax import lax
from jax.experimental import pallas as pl
from jax.experimental.pallas import tpu as pltpu

D = 1024
F = 4 * D
WINDOWS = (2, 4, 8, 16)
NG = len(WINDOWS)
G = D // NG
PSTATE = max(WINDOWS) - 1
CW = 3
CSTATE = CW - 1
NMETA = 16
EPS = 1e-5

SEG = 64
NSEG = 8
TM = SEG * NSEG
HP = 16
FC = 1024
NCH = F // FC
WN = 512
WEIGHT_BYTES_MAX = 2 * (3 * D * D + D * D + 2 * D * F)
TILE_BYTES = 4 * TM * D
VMEM_LIMIT = WEIGHT_BYTES_MAX + 12 * TILE_BYTES

f32 = jnp.float32
bf16 = jnp.bfloat16


def _rmsnorm(x, g):
    ms = jnp.mean(x * x, axis=-1, keepdims=True)
    return x * lax.rsqrt(ms + EPS) * g


def _weight_blocks(w_hbm, w_scr, lead=()):
    k, n = w_hbm.shape[len(lead):]
    ppb = D // WN
    return [(w_hbm.at[lead + (pl.ds(r * TM, TM), pl.ds(c * D, D))],
             [w_scr.at[c * ppb + p, pl.ds(r * TM, TM), :] for p in range(ppb)])
            for c in range(n // D) for r in range(k // TM)]


def _stage_weights(blocks, stg_ref, sem_ref):
    def copy(k):
        return pltpu.make_async_copy(blocks[k][0], stg_ref.at[k % 2], sem_ref.at[k % 2])

    copy(0).start()
    for k, (_, panels) in enumerate(blocks):
        if k + 1 < len(blocks):
            copy(k + 1).start()
        copy(k).wait()
        for p, dst in enumerate(panels):
            dst[...] = stg_ref[k % 2, :, p * WN:(p + 1) * WN].astype(bf16)


class _TileIO:
    def __init__(self, ns_tiles, nt_tiles, xs_hbm, xp_hbm, ys_hbm, yp_hbm,
                 xin_ref, yout_ref, sem_in, sem_out):
        self.ns, self.nt = ns_tiles, nt_tiles
        self.src, self.dst = (xs_hbm, xp_hbm), (ys_hbm, yp_hbm)
        self.xin, self.yout, self.sem_in, self.sem_out = xin_ref, yout_ref, sem_in, sem_out

    def _in(self, kind, idx, slot):
        return [pltpu.make_async_copy(self.src[kind].at[idx, b], self.xin.at[slot, :, b, :],
                                      self.sem_in.at[slot]) for b in range(NSEG)]

    def _out(self, kind, idx, slot):
        return [pltpu.make_async_copy(self.yout.at[slot, :, b, :], self.dst[kind].at[idx, b],
                                      self.sem_out.at[slot]) for b in range(NSEG)]

    def _for_tile(self, make, tile, slot, action):
        @pl.when(tile < self.ns)
        def _():
            for c in make(0, tile, slot):
                action(c)

        @pl.when(jnp.logical_and(tile >= self.ns, tile < self.nt))
        def _():
            for c in make(1, tile - self.ns, slot):
                action(c)

    def start_fetch(self, tile):
        self._for_tile(self._in, tile, tile % 2, lambda c: c.start())

    def wait_fetch(self, tile):
        for c in self._in(1, 0, tile % 2):
            c.wait()

    def start_store(self, tile):
        self._for_tile(self._out, tile, tile % 2, lambda c: c.start())

    def wait_store(self, tile):
        for c in self._out(1, 0, tile % 2):
            c.wait()


def _dot_panels(x, w_ref, panels, rows=slice(None)):
    return jnp.concatenate(
        [jnp.dot(x, w_ref[j, rows, :], preferred_element_type=f32) for j in panels], axis=1)


def _mlp(x, g, wup_ref, wdn_ref):
    hn = _rmsnorm(x, g).astype(bf16)
    ppc = FC // WN
    acc = x
    for c in range(NCH):
        a = _dot_panels(hn, wup_ref, range(c * ppc, (c + 1) * ppc))
        a = jnp.maximum(a, 0.0)
        a = (a * a).astype(bf16)
        acc = acc + _dot_panels(a, wdn_ref, range(D // WN), slice(c * FC, (c + 1) * FC))
    return acc


def _shift_steps(x, k, stride):
    return pltpu.roll(x, k * stride, 0)


def _window_sums(e, stride):
    out = []
    for g, w in enumerate(WINDOWS):
        s = e[:, g * G:(g + 1) * G]
        k = 1
        while k < w:
            s = s + _shift_steps(s, k, stride)
            k *= 2
        out.append(s)
    return out


def _prev_segment(tail, carry):
    steps = tail.shape[0] // NSEG
    t3 = pltpu.roll(tail.reshape(steps, NSEG, D), 1, 1)
    c3 = pltpu.roll(carry.reshape(steps, NSEG, D), 1, 1)
    first = lax.broadcasted_iota(jnp.int32, (steps, NSEG, D), 1) == 0
    return jnp.where(first, c3, t3).reshape(steps * NSEG, D)


def _pool_proj(x, db, wpool_ref, pscale):
    parts = [jnp.dot(db[:, g * G:(g + 1) * G], wpool_ref[g], preferred_element_type=f32)
             for g in range(NG)]
    return x + jnp.concatenate(parts, axis=1) * pscale


def _run_tile(t, nt_tiles, io, compute):
    io.wait_fetch(t)
    io.start_fetch(t + 1)
    compute(t % 2)
    io.start_store(t)

    @pl.when(t >= 1)
    def _():
        io.wait_store(t - 1)

    @pl.when(t == nt_tiles - 1)
    def _():
        io.wait_store(t)


def _layer0_kernel(ns_tiles, nt_tiles,
                   xs_hbm, xp_hbm, sp_ref, meta_ref, gmix_ref, gmlp_ref,
                   wpool_f32_ref, pscale_ref, wup_hbm, wdn_hbm,
                   os_hbm, op_hbm, ometa_ref, nps_ref, npp_ref,
                   xin_ref, yout_ref, st_ref, hc_ref, cmeta_ref,
                   wpool_ref, wup_ref, wdn_ref, stg_ref, sem_in, sem_out, sem_w):
    t = pl.program_id(0)
    is_sample = t < ns_tiles
    gmix = gmix_ref[0:1, :]
    gmlp = gmlp_ref[0:1, :]
    pscale = pscale_ref[...]
    io = _TileIO(ns_tiles, nt_tiles, xs_hbm, xp_hbm, os_hbm, op_hbm,
                 xin_ref, yout_ref, sem_in, sem_out)

    @pl.when(t == 0)
    def _init():
        io.start_fetch(t)
        _stage_weights(_weight_blocks(wup_hbm, wup_ref, (0,)) +
                       _weight_blocks(wdn_hbm, wdn_ref, (0,)), stg_ref, sem_w)
        wpool_ref[...] = wpool_f32_ref[...].astype(bf16)
        xm = meta_ref[...]
        hm = _rmsnorm(xm, gmix)
        pos1 = lax.broadcasted_iota(jnp.int32, (NMETA, 1), 0) + 1
        sums = _window_sums(jnp.concatenate([jnp.zeros((HP, D), f32), hm], axis=0), 1)
        cols = []
        for g, w in enumerate(WINDOWS):
            cnt = jnp.minimum(pos1, w).astype(f32)
            cols.append(sums[g][HP:] / cnt - hm[:, g * G:(g + 1) * G])
        d = jnp.concatenate(cols, axis=1).astype(bf16)
        ometa_ref[...] = _mlp(_pool_proj(xm, d, wpool_ref, pscale), gmlp, wup_ref, wdn_ref)
        cmeta_ref[...] = jnp.zeros(cmeta_ref.shape, f32)
        for k in range(HP):
            cmeta_ref[k * NSEG + NSEG - 1:(k + 1) * NSEG, :] = hm[NMETA - HP + k:NMETA - HP + k + 1]
        st_ref[...] = jnp.zeros(st_ref.shape, f32)
        hc_ref[...] = jnp.zeros(hc_ref.shape, f32)

    def compute(slot):
        x = xin_ref[slot].reshape(TM, D)
        h = _rmsnorm(x, gmix)
        tail = h[TM - HP * NSEG:TM]
        carry = jnp.where(t == ns_tiles, cmeta_ref[...], hc_ref[...])
        hc_ref[...] = tail
        st_ref[NSEG:HP * NSEG, :] = sp_ref[...].reshape(PSTATE * NSEG, D)
        hist = jnp.where(is_sample, st_ref[...], _prev_segment(tail, carry))
        sums = _window_sums(jnp.concatenate([hist, h], axis=0), NSEG)
        d = jnp.concatenate(
            [sums[g][HP * NSEG:] * (1.0 / w) - h[:, g * G:(g + 1) * G]
             for g, w in enumerate(WINDOWS)], axis=1).astype(bf16)
        x1 = _pool_proj(x, d, wpool_ref, pscale)
        yout_ref[slot] = _mlp(x1, gmlp, wup_ref, wdn_ref).reshape(SEG, NSEG, D)

    _run_tile(t, nt_tiles, io, compute)

    @pl.when(is_sample)
    def _store_sample_state():
        nps_ref[...] = hc_ref[(HP - PSTATE) * NSEG:HP * NSEG, :].reshape(PSTATE, NSEG, D)

    @pl.when(t == nt_tiles - 1)
    def _store_prompt_state():
        for k in range(PSTATE):
            r = (HP - PSTATE + k) * NSEG + NSEG - 1
            npp_ref[k] = hc_ref[r:r + 1, :]


def _layer1_kernel(ns_tiles, nt_tiles,
                   xs_hbm, xp_hbm, sc_ref, xmeta_ref, gmix_ref, gmlp_ref, gfin_ref,
                   wci_hbm, cw_ref, wco_hbm, wup_hbm, wdn_hbm,
                   ys_hbm, yp_hbm, ncs_ref, ncp_ref,
                   xin_ref, yout_ref, uc_ref, cmeta_ref,
                   wci_ref, wco_ref, wup_ref, wdn_ref, stg_ref, sem_in, sem_out, sem_w):
    t = pl.program_id(0)
    is_sample = t < ns_tiles
    gmix = gmix_ref[1:2, :]
    cw = cw_ref[...]
    io = _TileIO(ns_tiles, nt_tiles, xs_hbm, xp_hbm, ys_hbm, yp_hbm,
                 xin_ref, yout_ref, sem_in, sem_out)

    def conv_in(h1b, k):
        ppd = D // WN
        return _dot_panels(h1b, wci_ref, range(k * ppd, (k + 1) * ppd))

    @pl.when(t == 0)
    def _init():
        io.start_fetch(t)
        _stage_weights(_weight_blocks(wci_hbm, wci_ref) + _weight_blocks(wco_hbm, wco_ref) +
                       _weight_blocks(wup_hbm, wup_ref, (1,)) +
                       _weight_blocks(wdn_hbm, wdn_ref, (1,)), stg_ref, sem_w)
        hm = _rmsnorm(xmeta_ref[...], gmix).astype(bf16)
        um = conv_in(hm, 1) * conv_in(hm, 2)
        cmeta_ref[...] = jnp.zeros(cmeta_ref.shape, f32)
        for k in range(CSTATE):
            cmeta_ref[k * NSEG + NSEG - 1:(k + 1) * NSEG, :] = um[NMETA - CSTATE + k:NMETA - CSTATE + k + 1]
        uc_ref[...] = jnp.zeros(uc_ref.shape, f32)

    def compute(slot):
        x = xin_ref[slot].reshape(TM, D)
        h1b = _rmsnorm(x, gmix).astype(bf16)
        u = conv_in(h1b, 1) * conv_in(h1b, 2)
        tail = u[TM - CSTATE * NSEG:TM]
        carry = jnp.where(t == ns_tiles, cmeta_ref[...], uc_ref[...])
        uc_ref[...] = tail
        state = jnp.concatenate([sc_ref[:, k, :] for k in range(CSTATE)], axis=0)
        hist = jnp.where(is_sample, state, _prev_segment(tail, carry))
        e = jnp.concatenate([hist, u], axis=0)
        y = cw[CW - 1:CW, :] * u
        for k in range(1, CW):
            first = (CSTATE - k) * NSEG
            y = y + cw[CW - 1 - k:CW - k, :] * e[first:first + TM]
        gated = (conv_in(h1b, 0) * y).astype(bf16)
        x3 = x + _dot_panels(gated, wco_ref, range(D // WN))
        out = _rmsnorm(_mlp(x3, gmlp_ref[1:2, :], wup_ref, wdn_ref), gfin_ref[...])
        yout_ref[slot] = out.reshape(SEG, NSEG, D)

    _run_tile(t, nt_tiles, io, compute)

    @pl.when(is_sample)
    def _store_sample_state():
        for k in range(CSTATE):
            ncs_ref[:, k, :] = uc_ref[k * NSEG:(k + 1) * NSEG, :]

    @pl.when(t == nt_tiles - 1)
    def _store_prompt_state():
        for k in range(CSTATE):
            r = k * NSEG + NSEG - 1
            ncp_ref[0, k:k + 1, :] = uc_ref[r:r + 1, :]


def _const_spec(shape, single=False):
    nd = len(shape)
    kw = {"pipeline_mode": pl.Buffered(1)} if single else {}
    return pl.BlockSpec(shape, lambda i: (0,) * nd, **kw)


_HBM_SPEC = pl.BlockSpec(memory_space=pl.ANY)


def _panel_scratch(k, n):
    return pltpu.VMEM((n // WN, k, WN), bf16)


_TILE_BUFS = [pltpu.VMEM((2, SEG, NSEG, D), f32), pltpu.VMEM((2, SEG, NSEG, D), f32)]
_STAGING = [pltpu.VMEM((2, TM, D), f32)]
_SEMS = [pltpu.SemaphoreType.DMA((2,))] * 3


def kernel(x_prompt, x_sample, state_pool, state_conv, meta_tokens, norm_mix, norm_mlp,
           norm_final, w_pool, pool_scale, w_conv_in, conv_w, w_conv_out, w_up, w_down):
    bsz, seq, _ = x_prompt.shape
    nb, dseq, _ = x_sample.shape
    assert bsz == 1 and dseq == SEG and seq % TM == 0 and nb % NSEG == 0
    np_tiles = seq // TM
    ns_tiles = nb // NSEG
    nt_tiles = ns_tiles + np_tiles
    nlayers = norm_mix.shape[0]

    xs = x_sample.reshape(ns_tiles, NSEG, SEG, D)
    xp = x_prompt.reshape(np_tiles, NSEG, SEG, D)
    s_tiles = jax.ShapeDtypeStruct(xs.shape, f32)
    p_tiles = jax.ShapeDtypeStruct(xp.shape, f32)
    row = lambda a: a.reshape(1, D)
    st_idx = lambda t: (0, jnp.minimum(t, ns_tiles - 1), 0)
    cs_idx = lambda t: (jnp.minimum(t, ns_tiles - 1), 0, 0)
    pool_state_t = jnp.swapaxes(state_pool, 0, 1)
    params = pltpu.CompilerParams(dimension_semantics=("arbitrary",),
                                  vmem_limit_bytes=VMEM_LIMIT)

    x1s, x1p, xmeta1, new_pool_s, new_pool_p = pl.pallas_call(
        functools.partial(_layer0_kernel, ns_tiles, nt_tiles),
        grid=(nt_tiles,),
        in_specs=[_HBM_SPEC, _HBM_SPEC,
                  pl.BlockSpec((PSTATE, NSEG, D), st_idx),
                  _const_spec((NMETA, D)), _const_spec((nlayers, D)), _const_spec((nlayers, D)),
                  _const_spec((NG, G, G), single=True), _const_spec((1, D)),
                  _HBM_SPEC, _HBM_SPEC],
        out_specs=[_HBM_SPEC, _HBM_SPEC, _const_spec((NMETA, D)),
                   pl.BlockSpec((PSTATE, NSEG, D), st_idx),
                   _const_spec((PSTATE, 1, D))],
        out_shape=[s_tiles, p_tiles,
                   jax.ShapeDtypeStruct((NMETA, D), f32),
                   jax.ShapeDtypeStruct((PSTATE, nb, D), f32),
                   jax.ShapeDtypeStruct((PSTATE, 1, D), f32)],
        scratch_shapes=_TILE_BUFS +
                       [pltpu.VMEM((HP * NSEG, D), f32),
                        pltpu.VMEM((HP * NSEG, D), f32),
                        pltpu.VMEM((HP * NSEG, D), f32),
                        pltpu.VMEM((NG, G, G), bf16),
                        _panel_scratch(D, F), _panel_scratch(F, D)] + _STAGING + _SEMS,
        compiler_params=params,
        name="layer0_pool_mlp",
    )(xs, xp, pool_state_t, meta_tokens, norm_mix, norm_mlp,
      w_pool, row(pool_scale), w_up, w_down)

    ys, yp, new_conv_s, new_conv_p = pl.pallas_call(
        functools.partial(_layer1_kernel, ns_tiles, nt_tiles),
        grid=(nt_tiles,),
        in_specs=[_HBM_SPEC, _HBM_SPEC,
                  pl.BlockSpec((NSEG, CSTATE, D), cs_idx),
                  _const_spec((NMETA, D)), _const_spec((nlayers, D)), _const_spec((nlayers, D)),
                  _const_spec((1, D)),
                  _HBM_SPEC, _const_spec((CW, D)), _HBM_SPEC, _HBM_SPEC, _HBM_SPEC],
        out_specs=[_HBM_SPEC, _HBM_SPEC,
                   pl.BlockSpec((NSEG, CSTATE, D), cs_idx),
                   _const_spec((1, CSTATE, D))],
        out_shape=[s_tiles, p_tiles,
                   jax.ShapeDtypeStruct((nb, CSTATE, D), f32),
                   jax.ShapeDtypeStruct((1, CSTATE, D), f32)],
        scratch_shapes=_TILE_BUFS +
                       [pltpu.VMEM((CSTATE * NSEG, D), f32),
                        pltpu.VMEM((CSTATE * NSEG, D), f32),
                        _panel_scratch(D, 3 * D), _panel_scratch(D, D),
                        _panel_scratch(D, F), _panel_scratch(F, D)] + _STAGING + _SEMS,
        compiler_params=params,
        name="layer1_conv_mlp",
    )(x1s, x1p, state_conv, xmeta1, norm_mix, norm_mlp, row(norm_final),
      w_conv_in, conv_w, w_conv_out, w_up, w_down)

    return (yp.reshape(1, seq, D), ys.reshape(nb, dseq, D),
            jnp.swapaxes(new_pool_p, 0, 1), jnp.swapaxes(new_pool_s, 0, 1),
            new_conv_p, new_conv_s)
```

```python
import functools

import jax
import jax.numpy as jnp
from jax import lax
from jax.experimental import pallas as pl
from jax.experimental.pallas import tpu as pltpu

D = 1024
F = 4 * D
WINDOWS = (2, 4, 8, 16)
NG = len(WINDOWS)
G = D // NG
PSTATE = max(WINDOWS) - 1
CW = 3
CSTATE = CW - 1
NMETA = 16
EPS = 1e-5

SEG = 64
NSEG0 = 16
NSEG1 = 8
SR = 512
HP = 16
FC = 1024
NCH = F // FC
WN = 512
VMEM_LIMIT = 60 * 1024 * 1024

f32 = jnp.float32
bf16 = jnp.bfloat16


def _rmsnorm(x, g):
    ms = jnp.mean(x * x, axis=-1, keepdims=True)
    return x * lax.rsqrt(ms + EPS) * g


def _weight_blocks(w_hbm, w_scr, lead=()):
    k, n = w_hbm.shape[len(lead):]
    ppb = D // WN
    return [(w_hbm.at[lead + (pl.ds(r * SR, SR), pl.ds(c * D, D))],
             [w_scr.at[c * ppb + p, pl.ds(r * SR, SR), :] for p in range(ppb)])
            for c in range(n // D) for r in range(k // SR)]


def _stage_weights(blocks, stg_ref, sem_ref):
    def copy(k):
        return pltpu.make_async_copy(blocks[k][0], stg_ref.at[k % 2], sem_ref.at[k % 2])

    copy(0).start()
    for k, (_, panels) in enumerate(blocks):
        if k + 1 < len(blocks):
            copy(k + 1).start()
        copy(k).wait()
        for p, dst in enumerate(panels):
            dst[...] = stg_ref[k % 2, :, p * WN:(p + 1) * WN].astype(bf16)


class _TileIO:
    def __init__(self, ns_tiles, nt_tiles, xs_hbm, xp_hbm, ys_hbm, yp_hbm,
                 xin_ref, yout_ref, sem_in, sem_out):
        self.ns, self.nt = ns_tiles, nt_tiles
        self.src, self.dst = (xs_hbm, xp_hbm), (ys_hbm, yp_hbm)
        self.xin, self.yout, self.sem_in, self.sem_out = xin_ref, yout_ref, sem_in, sem_out
        self.nseg = xin_ref.shape[2]

    def _in(self, kind, idx, slot):
        return [pltpu.make_async_copy(self.src[kind].at[idx, b], self.xin.at[slot, :, b, :],
                                      self.sem_in.at[slot]) for b in range(self.nseg)]

    def _out(self, kind, idx, slot):
        return [pltpu.make_async_copy(self.yout.at[slot, :, b, :], self.dst[kind].at[idx, b],
                                      self.sem_out.at[slot]) for b in range(self.nseg)]

    def _for_tile(self, make, tile, slot, action):
        @pl.when(tile < self.ns)
        def _():
            for c in make(0, tile, slot):
                action(c)

        @pl.when(jnp.logical_and(tile >= self.ns, tile < self.nt))
        def _():
            for c in make(1, tile - self.ns, slot):
                action(c)

    def start_fetch(self, tile):
        self._for_tile(self._in, tile, tile % 2, lambda c: c.start())

    def wait_fetch(self, tile):
        for c in self._in(1, 0, tile % 2):
            c.wait()

    def start_store(self, tile):
        self._for_tile(self._out, tile, tile % 2, lambda c: c.start())

    def wait_store(self, tile):
        for c in self._out(1, 0, tile % 2):
            c.wait()


def _dot_panels(x, w_ref, panels, rows=slice(None)):
    return jnp.concatenate(
        [jnp.dot(x, w_ref[j, rows, :], preferred_element_type=f32) for j in panels], axis=1)


def _mlp(x, g, wup_ref, wdn_ref):
    hn = _rmsnorm(x, g).astype(bf16)
    ppc = FC // WN
    acc = x
    for c in range(NCH):
        a = _dot_panels(hn, wup_ref, range(c * ppc, (c + 1) * ppc))
        a = jnp.maximum(a, 0.0)
        a = (a * a).astype(bf16)
        acc = acc + _dot_panels(a, wdn_ref, range(D // WN), slice(c * FC, (c + 1) * FC))
    return acc


def _shift_steps(x, k, stride):
    return pltpu.roll(x, k * stride, 0)


def _window_sums(e, stride):
    out = []
    for g, w in enumerate(WINDOWS):
        s = e[:, g * G:(g + 1) * G]
        k = 1
        while k < w:
            s = s + _shift_steps(s, k, stride)
            k *= 2
        out.append(s)
    return out


def _prev_segment(tail, carry, nseg):
    steps = tail.shape[0] // nseg
    t3 = pltpu.roll(tail.reshape(steps, nseg, D), 1, 1)
    c3 = pltpu.roll(carry.reshape(steps, nseg, D), 1, 1)
    first = lax.broadcasted_iota(jnp.int32, (steps, nseg, D), 1) == 0
    return jnp.where(first, c3, t3).reshape(steps * nseg, D)


def _pool_proj(x, db, wpool_ref, pscale):
    parts = [jnp.dot(db[:, g * G:(g + 1) * G], wpool_ref[g], preferred_element_type=f32)
             for g in range(NG)]
    return x + jnp.concatenate(parts, axis=1) * pscale


def _run_tile(t, nt_tiles, io, compute):
    io.wait_fetch(t)
    io.start_fetch(t + 1)
    compute(t % 2)
    io.start_store(t)

    @pl.when(t >= 1)
    def _():
        io.wait_store(t - 1)

    @pl.when(t == nt_tiles - 1)
    def _():
        io.wait_store(t)


def _layer0_kernel(ns_tiles, nt_tiles,
                   xs_hbm, xp_hbm, sp_ref, meta_ref, gmix_ref, gmlp_ref,
                   wpool_f32_ref, pscale_ref, wup_hbm, wdn_hbm,
                   os_hbm, op_hbm, ometa_ref, nps_ref, npp_ref,
                   xin_ref, yout_ref, st_ref, hc_ref, cmeta_ref,
                   wpool_ref, wup_ref, wdn_ref, stg_ref, sem_in, sem_out, sem_w):
    nseg = xin_ref.shape[2]
    tm = SEG * nseg
    t = pl.program_id(0)
    is_sample = t < ns_tiles
    gmix = gmix_ref[0:1, :]
    gmlp = gmlp_ref[0:1, :]
    pscale = pscale_ref[...]
    io = _TileIO(ns_tiles, nt_tiles, xs_hbm, xp_hbm, os_hbm, op_hbm,
                 xin_ref, yout_ref, sem_in, sem_out)

    @pl.when(t == 0)
    def _init():
        io.start_fetch(t)
        _stage_weights(_weight_blocks(wup_hbm, wup_ref, (0,)) +
                       _weight_blocks(wdn_hbm, wdn_ref, (0,)), stg_ref, sem_w)
        wpool_ref[...] = wpool_f32_ref[...].astype(bf16)
        xm = meta_ref[...]
        hm = _rmsnorm(xm, gmix)
        pos1 = lax.broadcasted_iota(jnp.int32, (NMETA, 1), 0) + 1
        sums = _window_sums(jnp.concatenate([jnp.zeros((HP, D), f32), hm], axis=0), 1)
        cols = []
        for g, w in enumerate(WINDOWS):
            cnt = jnp.minimum(pos1, w).astype(f32)
            cols.append(sums[g][HP:] / cnt - hm[:, g * G:(g + 1) * G])
        d = jnp.concatenate(cols, axis=1).astype(bf16)
        ometa_ref[...] = _mlp(_pool_proj(xm, d, wpool_ref, pscale), gmlp, wup_ref, wdn_ref)
        cmeta_ref[...] = jnp.zeros(cmeta_ref.shape, f32)
        for k in range(HP):
            cmeta_ref[k * nseg + nseg - 1:(k + 1) * nseg, :] = hm[NMETA - HP + k:NMETA - HP + k + 1]
        st_ref[...] = jnp.zeros(st_ref.shape, f32)
        hc_ref[...] = jnp.zeros(hc_ref.shape, f32)

    def compute(slot):
        x = xin_ref[slot].reshape(tm, D)
        h = _rmsnorm(x, gmix)
        tail = h[tm - HP * nseg:tm]
        carry = jnp.where(t == ns_tiles, cmeta_ref[...], hc_ref[...])
        hc_ref[...] = tail
        st_ref[nseg:HP * nseg, :] = sp_ref[...].reshape(PSTATE * nseg, D)
        hist = jnp.where(is_sample, st_ref[...], _prev_segment(tail, carry, nseg))
        sums = _window_sums(jnp.concatenate([hist, h], axis=0), nseg)
        d = jnp.concatenate(
            [sums[g][HP * nseg:] * (1.0 / w) - h[:, g * G:(g + 1) * G]
             for g, w in enumerate(WINDOWS)], axis=1).astype(bf16)
        x1 = _pool_proj(x, d, wpool_ref, pscale)
        yout_ref[slot] = _mlp(x1, gmlp, wup_ref, wdn_ref).reshape(SEG, nseg, D)

    _run_tile(t, nt_tiles, io, compute)

    @pl.when(is_sample)
    def _store_sample_state():
        nps_ref[...] = hc_ref[(HP - PSTATE) * nseg:HP * nseg, :].reshape(PSTATE, nseg, D)

    @pl.when(t == nt_tiles - 1)
    def _store_prompt_state():
        for k in range(PSTATE):
            r = (HP - PSTATE + k) * nseg + nseg - 1
            npp_ref[k] = hc_ref[r:r + 1, :]


def _layer1_kernel(ns_tiles, nt_tiles,
                   xs_hbm, xp_hbm, sc_ref, xmeta_ref, gmix_ref, gmlp_ref, gfin_ref,
                   wci_hbm, cw_ref, wco_hbm, wup_hbm, wdn_hbm,
                   ys_hbm, yp_hbm, ncs_ref, ncp_ref,
                   xin_ref, yout_ref, uc_ref, cmeta_ref,
                   wci_ref, wco_ref, wup_ref, wdn_ref, stg_ref, sem_in, sem_out, sem_w):
    nseg = xin_ref.shape[2]
    tm = SEG * nseg
    t = pl.program_id(0)
    is_sample = t < ns_tiles
    gmix = gmix_ref[1:2, :]
    cw = cw_ref[...]
    io = _TileIO(ns_tiles, nt_tiles, xs_hbm, xp_hbm, ys_hbm, yp_hbm,
                 xin_ref, yout_ref, sem_in, sem_out)

    def conv_in(h1b, k):
        ppd = D // WN
        return _dot_panels(h1b, wci_ref, range(k * ppd, (k + 1) * ppd))

    @pl.when(t == 0)
    def _init():
        io.start_fetch(t)
        _stage_weights(_weight_blocks(wci_hbm, wci_ref) + _weight_blocks(wco_hbm, wco_ref) +
                       _weight_blocks(wup_hbm, wup_ref, (1,)) +
                       _weight_blocks(wdn_hbm, wdn_ref, (1,)), stg_ref, sem_w)
        hm = _rmsnorm(xmeta_ref[...], gmix).astype(bf16)
        um = conv_in(hm, 1) * conv_in(hm, 2)
        cmeta_ref[...] = jnp.zeros(cmeta_ref.shape, f32)
        for k in range(CSTATE):
            cmeta_ref[k * nseg + nseg - 1:(k + 1) * nseg, :] = um[NMETA - CSTATE + k:NMETA - CSTATE + k + 1]
        uc_ref[...] = jnp.zeros(uc_ref.shape, f32)

    def compute(slot):
        x = xin_ref[slot].reshape(tm, D)
        h1b = _rmsnorm(x, gmix).astype(bf16)
        u = conv_in(h1b, 1) * conv_in(h1b, 2)
        tail = u[tm - CSTATE * nseg:tm]
        carry = jnp.where(t == ns_tiles, cmeta_ref[...], uc_ref[...])
        uc_ref[...] = tail
        state = jnp.concatenate([sc_ref[:, k, :] for k in range(CSTATE)], axis=0)
        hist = jnp.where(is_sample, state, _prev_segment(tail, carry, nseg))
        e = jnp.concatenate([hist, u], axis=0)
        y = cw[CW - 1:CW, :] * u
        for k in range(1, CW):
            first = (CSTATE - k) * nseg
            y = y + cw[CW - 1 - k:CW - k, :] * e[first:first + tm]
        gated = (conv_in(h1b, 0) * y).astype(bf16)
        x3 = x + _dot_panels(gated, wco_ref, range(D // WN))
        out = _rmsnorm(_mlp(x3, gmlp_ref[1:2, :], wup_ref, wdn_ref), gfin_ref[...])
        yout_ref[slot] = out.reshape(SEG, nseg, D)

    _run_tile(t, nt_tiles, io, compute)

    @pl.when(is_sample)
    def _store_sample_state():
        for k in range(CSTATE):
            ncs_ref[:, k, :] = uc_ref[k * nseg:(k + 1) * nseg, :]

    @pl.when(t == nt_tiles - 1)
    def _store_prompt_state():
        for k in range(CSTATE):
            r = k * nseg + nseg - 1
            ncp_ref[0, k:k + 1, :] = uc_ref[r:r + 1, :]


def _const_spec(shape, single=False):
    nd = len(shape)
    kw = {"pipeline_mode": pl.Buffered(1)} if single else {}
    return pl.BlockSpec(shape, lambda i: (0,) * nd, **kw)


_HBM_SPEC = pl.BlockSpec(memory_space=pl.ANY)


def _panel_scratch(k, n):
    return pltpu.VMEM((n // WN, k, WN), bf16)


def _tile_bufs(nseg):
    return [pltpu.VMEM((2, SEG, nseg, D), f32), pltpu.VMEM((2, SEG, nseg, D), f32)]


_STAGING = [pltpu.VMEM((2, SR, D), f32)]
_SEMS = [pltpu.SemaphoreType.DMA((2,))] * 3


def kernel(x_prompt, x_sample, state_pool, state_conv, meta_tokens, norm_mix, norm_mlp,
           norm_final, w_pool, pool_scale, w_conv_in, conv_w, w_conv_out, w_up, w_down):
    bsz, seq, _ = x_prompt.shape
    nb, dseq, _ = x_sample.shape
    assert bsz == 1 and dseq == SEG
    nlayers = norm_mix.shape[0]
    row = lambda a: a.reshape(1, D)
    params = pltpu.CompilerParams(dimension_semantics=("arbitrary",),
                                  vmem_limit_bytes=VMEM_LIMIT)

    def tiling(nseg):
        assert seq % (SEG * nseg) == 0 and nb % nseg == 0
        ns_tiles, np_tiles = nb // nseg, seq // (SEG * nseg)
        return ns_tiles, np_tiles, (lambda t: jnp.minimum(t, ns_tiles - 1))

    def tiles(x, n_tiles, nseg):
        return x.reshape(n_tiles, nseg, SEG, D)

    nseg = NSEG0
    ns_tiles, np_tiles, st_tile = tiling(nseg)
    st_idx = lambda t: (0, st_tile(t), 0)
    xs, xp = tiles(x_sample, ns_tiles, nseg), tiles(x_prompt, np_tiles, nseg)
    x1s, x1p, xmeta1, new_pool_s, new_pool_p = pl.pallas_call(
        functools.partial(_layer0_kernel, ns_tiles, ns_tiles + np_tiles),
        grid=(ns_tiles + np_tiles,),
        in_specs=[_HBM_SPEC, _HBM_SPEC,
                  pl.BlockSpec((PSTATE, nseg, D), st_idx),
                  _const_spec((NMETA, D)), _const_spec((nlayers, D)), _const_spec((nlayers, D)),
                  _const_spec((NG, G, G), single=True), _const_spec((1, D)),
                  _HBM_SPEC, _HBM_SPEC],
        out_specs=[_HBM_SPEC, _HBM_SPEC, _const_spec((NMETA, D)),
                   pl.BlockSpec((PSTATE, nseg, D), st_idx),
                   _const_spec((PSTATE, 1, D))],
        out_shape=[jax.ShapeDtypeStruct(xs.shape, f32), jax.ShapeDtypeStruct(xp.shape, f32),
                   jax.ShapeDtypeStruct((NMETA, D), f32),
                   jax.ShapeDtypeStruct((PSTATE, nb, D), f32),
                   jax.ShapeDtypeStruct((PSTATE, 1, D), f32)],
        scratch_shapes=_tile_bufs(nseg) +
                       [pltpu.VMEM((HP * nseg, D), f32),
                        pltpu.VMEM((HP * nseg, D), f32),
                        pltpu.VMEM((HP * nseg, D), f32),
                        pltpu.VMEM((NG, G, G), bf16),
                        _panel_scratch(D, F), _panel_scratch(F, D)] + _STAGING + _SEMS,
        compiler_params=params,
        name="layer0_pool_mlp",
    )(xs, xp, jnp.swapaxes(state_pool, 0, 1), meta_tokens, norm_mix, norm_mlp,
      w_pool, row(pool_scale), w_up, w_down)

    nseg = NSEG1
    ns_tiles, np_tiles, cs_tile = tiling(nseg)
    cs_idx = lambda t: (cs_tile(t), 0, 0)
    ys, yp, new_conv_s, new_conv_p = pl.pallas_call(
        functools.partial(_layer1_kernel, ns_tiles, ns_tiles + np_tiles),
        grid=(ns_tiles + np_tiles,),
        in_specs=[_HBM_SPEC, _HBM_SPEC,
                  pl.BlockSpec((nseg, CSTATE, D), cs_idx),
                  _const_spec((NMETA, D)), _const_spec((nlayers, D)), _const_spec((nlayers, D)),
                  _const_spec((1, D)),
                  _HBM_SPEC, _const_spec((CW, D)), _HBM_SPEC, _HBM_SPEC, _HBM_SPEC],
        out_specs=[_HBM_SPEC, _HBM_SPEC,
                   pl.BlockSpec((nseg, CSTATE, D), cs_idx),
                   _const_spec((1, CSTATE, D))],
        out_shape=[jax.ShapeDtypeStruct((ns_tiles, nseg, SEG, D), f32),
                   jax.ShapeDtypeStruct((np_tiles, nseg, SEG, D), f32),
                   jax.ShapeDtypeStruct((nb, CSTATE, D), f32),
                   jax.ShapeDtypeStruct((1, CSTATE, D), f32)],
        scratch_shapes=_tile_bufs(nseg) +
                       [pltpu.VMEM((CSTATE * nseg, D), f32),
                        pltpu.VMEM((CSTATE * nseg, D), f32),
                        _panel_scratch(D, 3 * D), _panel_scratch(D, D),
                        _panel_scratch(D, F), _panel_scratch(F, D)] + _STAGING + _SEMS,
        compiler_params=params,
        name="layer1_conv_mlp",
    )(tiles(x1s, ns_tiles, nseg), tiles(x1p, np_tiles, nseg), state_conv, xmeta1,
      norm_mix, norm_mlp, row(norm_final), w_conv_in, conv_w, w_conv_out, w_up, w_down)

    return (yp.reshape(1, seq, D), ys.reshape(nb, dseq, D),
            jnp.swapaxes(new_pool_p, 0, 1), jnp.swapaxes(new_pool_s, 0, 1),
            new_conv_p, new_conv_s)
```
